```python
import functools
import jax, jax.numpy as jnp
from jax import lax
import numpy as np

D_MODEL = 1024
BATCH = 4
SEQ = 8192
DEPTH = 2
DEC_BATCH = 32
DEC_SEQ = 4
PAST_LEN = 16384
PAGE_SIZE = 128

HEAD_DIM = 64
ATTN_WIDTH = D_MODEL // 2
N_ATTN_HEADS = ATTN_WIDTH // HEAD_DIM
POOL_WIDTH = D_MODEL // 4
POOL_WINDOWS = (2, 4, 8, 16)
N_POOL_GROUPS = len(POOL_WINDOWS)
POOL_GROUP_DIM = POOL_WIDTH // N_POOL_GROUPS
CONV_WIDTH = D_MODEL - ATTN_WIDTH - POOL_WIDTH
CONV_KERNEL = 31
MIX_WIDTH = ATTN_WIDTH + POOL_WIDTH + CONV_WIDTH
IN_COLS = 3 * ATTN_WIDTH + N_ATTN_HEADS + POOL_WIDTH + 2 * CONV_WIDTH
POOL_HIST = max(POOL_WINDOWS) - 1
CONV_HIST = CONV_KERNEL - 1
D_FF = 2816
N_EXPERTS = 8
TOP_K = 2
N_DENSE = (DEPTH + 1) // 2
N_MOE = DEPTH // 2
Q_BLOCK = 128
ATTN_SCALE = HEAD_DIM ** -0.5
FORGET_BIAS = 3.0
RMS_EPS = 1e-6
LN_EPS = 1e-5

kernel_name = 'fox_pool_conformer_hybrid_step'

f32 = jnp.float32


def rms_norm(x, g):
    xf = x.astype(f32)
    y = xf * lax.rsqrt(jnp.mean(xf * xf, axis=-1, keepdims=True) + RMS_EPS)
    return (y * g.astype(f32)).astype(x.dtype)


def layer_norm(x, g, b):
    xf = x.astype(f32)
    mu = jnp.mean(xf, axis=-1, keepdims=True)
    var = jnp.mean(jnp.square(xf - mu), axis=-1, keepdims=True)
    return (xf - mu) * lax.rsqrt(var + LN_EPS) * g.astype(f32) + b.astype(f32)


def in_proj(h, w_in, b_f):
    z = jnp.einsum('btd,dc->btc', h, w_in)
    B, T, _ = z.shape
    A = ATTN_WIDTH
    q = z[..., :A].reshape(B, T, N_ATTN_HEADS, HEAD_DIM)
    k = z[..., A:2 * A].reshape(B, T, N_ATTN_HEADS, HEAD_DIM)
    v = z[..., 2 * A:3 * A].reshape(B, T, N_ATTN_HEADS, HEAD_DIM)
    o = 3 * A
    logf = jax.nn.log_sigmoid(z[..., o:o + N_ATTN_HEADS].astype(f32) + b_f.astype(f32))
    o += N_ATTN_HEADS
    u = z[..., o:o + POOL_WIDTH]
    o += POOL_WIDTH
    a = z[..., o:o + CONV_WIDTH] * jax.nn.sigmoid(z[..., o + CONV_WIDTH:o + 2 * CONV_WIDTH])
    return q, k, v, logf, u, a


def fox_block(q, cq, q_pos, k, v, ck):
    s = jnp.einsum('bqhd,bkhd->bhqk', q, k, preferred_element_type=f32) * ATTN_SCALE
    s = s + (jnp.swapaxes(cq, 1, 2)[..., :, None] - jnp.swapaxes(ck, 1, 2)[..., None, :])
    causal = jnp.arange(k.shape[1])[None, :] <= q_pos[:, None]
    s = jnp.where(causal, s, -jnp.inf)
    p = jax.nn.softmax(s, axis=-1)
    return jnp.einsum('bhqk,bkhd->bqhd', p.astype(v.dtype), v)


def fox_prompt(q, k, v, logf):
    B, S, H, D = q.shape
    c = jnp.cumsum(logf.astype(f32), axis=1)
    nb = S // Q_BLOCK
    qb = q.reshape(B, nb, Q_BLOCK, H, D).transpose(1, 0, 2, 3, 4)
    cb = c.reshape(B, nb, Q_BLOCK, H).transpose(1, 0, 2, 3)
    pos = jnp.arange(S, dtype=jnp.int32).reshape(nb, Q_BLOCK)
    out = lax.map(lambda blk: fox_block(blk[0], blk[1], blk[2], k, v, c), (qb, cb, pos))
    return out.transpose(1, 0, 2, 3, 4).reshape(B, S, H * D)


def fox_sample(q, k_new, v_new, logf_new, k_pool, v_pool, lf_pool, page_table):
    Bd, T, H, D = q.shape
    kp = k_pool[page_table].reshape(Bd, -1, H, D)
    vp = v_pool[page_table].reshape(Bd, -1, H, D)
    lfp = lf_pool[page_table].reshape(Bd, -1, H).astype(f32)
    P = kp.shape[1]
    c_past = jnp.cumsum(lfp, axis=1)
    c_new = c_past[:, -1:] + jnp.cumsum(logf_new.astype(f32), axis=1)
    cq = jnp.swapaxes(c_new, 1, 2)[..., :, None]
    s_past = jnp.einsum('bqhd,bkhd->bhqk', q, kp, preferred_element_type=f32) * ATTN_SCALE
    s_past = s_past + (cq - jnp.swapaxes(c_past, 1, 2)[..., None, :])
    s_new = jnp.einsum('bqhd,bkhd->bhqk', q, k_new, preferred_element_type=f32) * ATTN_SCALE
    s_new = s_new + (cq - jnp.swapaxes(c_new, 1, 2)[..., None, :])
    causal = jnp.arange(T)[None, :] <= jnp.arange(T)[:, None]
    s_new = jnp.where(causal, s_new, -jnp.inf)
    p = jax.nn.softmax(jnp.concatenate([s_past, s_new], axis=-1), axis=-1).astype(v_new.dtype)
    o = (jnp.einsum('bhqk,bkhd->bqhd', p[..., :P], vp)
         + jnp.einsum('bhqk,bkhd->bqhd', p[..., P:], v_new))
    return o.reshape(Bd, T, H * D)


def pool_mix(u, hist, pos0, pool_w, pool_scale):
    B, T, _ = u.shape
    ext = jnp.concatenate([hist.astype(u.dtype), u], axis=1)
    cs = jnp.cumsum(ext.astype(f32), axis=1)
    cs = jnp.concatenate([jnp.zeros_like(cs[:, :1]), cs], axis=1)
    pos = pos0 + jnp.arange(T)
    means = []
    for g, w in enumerate(POOL_WINDOWS):
        ch = slice(g * POOL_GROUP_DIM, (g + 1) * POOL_GROUP_DIM)
        win = cs[:, POOL_HIST + 1:POOL_HIST + 1 + T, ch] - cs[:, POOL_HIST + 1 - w:POOL_HIST + 1 - w + T, ch]
        cnt = jnp.minimum(pos + 1, w).astype(f32)[None, :, None]
        means.append(win / cnt)
    mean = jnp.stack(means, axis=2)
    d = mean - u.astype(f32).reshape(B, T, N_POOL_GROUPS, POOL_GROUP_DIM)
    y = jnp.einsum('btgc,gce->btge', d, pool_w.astype(f32)).reshape(B, T, POOL_WIDTH)
    y = y * pool_scale.astype(f32)
    return y.astype(u.dtype), ext[:, -POOL_HIST:]


def conv_mix(a, hist, conv_w, conv_b, ln_g, ln_b):
    ext = jnp.concatenate([hist.astype(a.dtype), a], axis=1)
    y = lax.conv_general_dilated(ext, conv_w[:, None, :].astype(a.dtype), window_strides=(1,),
                                 padding='VALID', dimension_numbers=('NWC', 'WIO', 'NWC'),
                                 feature_group_count=CONV_WIDTH)
    y = layer_norm(y + conv_b, ln_g, ln_b)
    return jax.nn.silu(y).astype(a.dtype), ext[:, -CONV_HIST:]


def mixing_sublayer(x, attend, pool_hist, conv_hist, pos0, g, w_in, b_f, pool_w, pool_scale,
                    conv_w, conv_b, ln_g, ln_b, w_out):
    h = rms_norm(x, g)
    q, k, v, logf, u, a = in_proj(h, w_in, b_f)
    y_att = attend(q, k, v, logf)
    y_pool, pool_new = pool_mix(u, pool_hist, pos0, pool_w, pool_scale)
    y_conv, conv_new = conv_mix(a, conv_hist, conv_w, conv_b, ln_g, ln_b)
    y = jnp.concatenate([y_att.astype(x.dtype), y_pool.astype(x.dtype), y_conv.astype(x.dtype)], axis=-1)
    return x + jnp.einsum('btc,cd->btd', y, w_out), (k, v, logf, pool_new, conv_new)


def swiglu(h, wg, wu, wd):
    hid = jax.nn.silu(jnp.einsum('btd,df->btf', h, wg)) * jnp.einsum('btd,df->btf', h, wu)
    return jnp.einsum('btf,fd->btd', hid, wd)


def moe_swiglu(h, w_router, wg, wu, wd):
    logits = jnp.einsum('btd,de->bte', h, w_router).astype(f32)
    top_v, top_i = lax.top_k(logits, TOP_K)
    gates = jax.nn.softmax(top_v, axis=-1)
    combine = jnp.sum(jax.nn.one_hot(top_i, N_EXPERTS, dtype=f32) * gates[..., None], axis=-2)
    y = jnp.zeros(h.shape, f32)
    for e in range(N_EXPERTS):
        y = y + combine[..., e:e + 1] * swiglu(h, wg[e], wu[e], wd[e]).astype(f32)
    return y.astype(h.dtype)


def channel_sublayer(x, l, norm_ffn, ffn_gate, ffn_up, ffn_down, moe_router, moe_gate, moe_up, moe_down):
    h = rms_norm(x, norm_ffn[l])
    i = l // 2
    if l % 2 == 0:
        y = swiglu(h, ffn_gate[i], ffn_up[i], ffn_down[i])
    else:
        y = moe_swiglu(h, moe_router[i], moe_gate[i], moe_up[i], moe_down[i])
    return x + y


def setup_inputs(seed: int = 0) -> dict:
    key = jax.random.key(seed)
    ks = jax.random.split(key, 32)
    n_pages = PAST_LEN // PAGE_SIZE
    n_used = DEC_BATCH * n_pages
    n_pool = n_used + max(1, n_used // 4)
    H, Dh = N_ATTN_HEADS, HEAD_DIM

    def nrm(k, shape, scale):
        return jax.random.normal(k, shape, f32) * scale

    page_table = jax.random.permutation(ks[0], n_pool)[:n_used].astype(jnp.int32).reshape(DEC_BATCH, n_pages)
    return {
        'x_prompt': nrm(ks[1], (BATCH, SEQ, D_MODEL), 1.0),
        'x_sample': nrm(ks[2], (DEC_BATCH, DEC_SEQ, D_MODEL), 1.0),
        'cache_k': nrm(ks[3], (DEPTH, n_pool, PAGE_SIZE, H, Dh), 1.0),
        'cache_v': nrm(ks[4], (DEPTH, n_pool, PAGE_SIZE, H, Dh), 1.0),
        'cache_logf': jax.nn.log_sigmoid(FORGET_BIAS + nrm(ks[5], (DEPTH, n_pool, PAGE_SIZE, H), 1.0)),
        'state_pool': nrm(ks[6], (DEPTH, DEC_BATCH, POOL_HIST, POOL_WIDTH), 1.0),
        'state_conv': nrm(ks[7], (DEPTH, DEC_BATCH, CONV_HIST, CONV_WIDTH), 0.5),
        'page_table': page_table,
        'norm_mix': 1.0 + nrm(ks[8], (DEPTH, D_MODEL), 0.02),
        'w_in': nrm(ks[9], (DEPTH, D_MODEL, IN_COLS), D_MODEL ** -0.5),
        'b_forget': FORGET_BIAS + nrm(ks[10], (DEPTH, H), 0.1),
        'pool_w': nrm(ks[11], (DEPTH, N_POOL_GROUPS, POOL_GROUP_DIM, POOL_GROUP_DIM), POOL_GROUP_DIM ** -0.5),
        'pool_scale': 1.0 + nrm(ks[12], (DEPTH, POOL_WIDTH), 0.02),
        'conv_w': nrm(ks[13], (DEPTH, CONV_KERNEL, CONV_WIDTH), CONV_KERNEL ** -0.5),
        'conv_b': nrm(ks[14], (DEPTH, CONV_WIDTH), 0.02),
        'conv_ln_g': 1.0 + nrm(ks[15], (DEPTH, CONV_WIDTH), 0.02),
        'conv_ln_b': nrm(ks[16], (DEPTH, CONV_WIDTH), 0.02),
        'w_out': nrm(ks[17], (DEPTH, MIX_WIDTH, D_MODEL), MIX_WIDTH ** -0.5),
        'norm_ffn': 1.0 + nrm(ks[18], (DEPTH, D_MODEL), 0.02),
        'ffn_gate': nrm(ks[19], (N_DENSE, D_MODEL, D_FF), D_MODEL ** -0.5),
        'ffn_up': nrm(ks[20], (N_DENSE, D_MODEL, D_FF), D_MODEL ** -0.5),
        'ffn_down': nrm(ks[21], (N_DENSE, D_FF, D_MODEL), D_FF ** -0.5),
        'moe_router': nrm(ks[22], (N_MOE, D_MODEL, N_EXPERTS), D_MODEL ** -0.5),
        'moe_gate': nrm(ks[23], (N_MOE, N_EXPERTS, D_MODEL, D_FF), D_MODEL ** -0.5),
        'moe_up': nrm(ks[24], (N_MOE, N_EXPERTS, D_MODEL, D_FF), D_MODEL ** -0.5),
        'moe_down': nrm(ks[25], (N_MOE, N_EXPERTS, D_FF, D_MODEL), D_FF ** -0.5),
        'norm_final': 1.0 + nrm(ks[26], (D_MODEL,), 0.02),
    }


def reference(x_prompt, x_sample, cache_k, cache_v, cache_logf, state_pool, state_conv, page_table,
              norm_mix, w_in, b_forget, pool_w, pool_scale, conv_w, conv_b, conv_ln_g, conv_ln_b, w_out,
              norm_ffn, ffn_gate, ffn_up, ffn_down, moe_router, moe_gate, moe_up, moe_down, norm_final):
    xp, xs = x_prompt, x_sample
    Bp = xp.shape[0]
    new_p = [[], [], [], [], []]
    new_s = [[], [], [], [], []]
    for l in range(DEPTH):
        mix_w = (norm_mix[l], w_in[l], b_forget[l], pool_w[l], pool_scale[l],
                 conv_w[l], conv_b[l], conv_ln_g[l], conv_ln_b[l], w_out[l])
        sample_attend = functools.partial(fox_sample, k_pool=cache_k[l], v_pool=cache_v[l],
                                          lf_pool=cache_logf[l], page_table=page_table)
        zero_pool = jnp.zeros((Bp, POOL_HIST, POOL_WIDTH), xp.dtype)
        zero_conv = jnp.zeros((Bp, CONV_HIST, CONV_WIDTH), xp.dtype)
        xp, st_p = mixing_sublayer(xp, fox_prompt, zero_pool, zero_conv, 0, *mix_w)
        xs, st_s = mixing_sublayer(xs, sample_attend, state_pool[l], state_conv[l], PAST_LEN, *mix_w)
        xp = channel_sublayer(xp, l, norm_ffn, ffn_gate, ffn_up, ffn_down, moe_router, moe_gate, moe_up, moe_down)
        xs = channel_sublayer(xs, l, norm_ffn, ffn_gate, ffn_up, ffn_down, moe_router, moe_gate, moe_up, moe_down)
        for j in range(5):
            new_p[j].append(st_p[j])
            new_s[j].append(st_s[j])
    y_prompt = rms_norm(xp, norm_final)
    y_sample = rms_norm(xs, norm_final)
    k_p, v_p, lf_p, pool_p, conv_p = [jnp.stack(a, axis=0) for a in new_p]
    k_s, v_s, lf_s, pool_s, conv_s = [jnp.stack(a, axis=0) for a in new_s]
    return (y_prompt, y_sample, k_p, v_p, lf_p, pool_p, conv_p, k_s, v_s, lf_s, pool_s, conv_s)
```

```python
import functools

import jax
import jax.numpy as jnp
from jax import lax
from jax.experimental import pallas as pl
from jax.experimental.pallas import tpu as pltpu

F32 = jnp.float32
BF16 = jnp.bfloat16

D_MODEL = 1024
HEAD_DIM = 64
ATTN_WIDTH = D_MODEL // 2
N_HEADS = ATTN_WIDTH // HEAD_DIM
POOL_WIDTH = D_MODEL // 4
POOL_WINDOWS = (2, 4, 8, 16)
POOL_GROUP_DIM = POOL_WIDTH // len(POOL_WINDOWS)
CONV_WIDTH = D_MODEL - ATTN_WIDTH - POOL_WIDTH
CONV_KERNEL = 31
POOL_HIST = max(POOL_WINDOWS) - 1
CONV_HIST = CONV_KERNEL - 1
N_EXPERTS = 8
ATTN_SCALE = HEAD_DIM ** -0.5
RMS_EPS = 1e-6
LN_EPS = 1e-5

LANES = 128
VMEM_LIMIT = 56 << 20

_QKV = 3 * ATTN_WIDTH
_COL_U = _QKV
_COL_A = _COL_U + POOL_WIDTH
_COL_G = _COL_A + CONV_WIDTH
_COL_F = _COL_G + CONV_WIDTH
_PACKED_COLS = _COL_F + LANES


def _params(n_axes, vmem=VMEM_LIMIT):
    return pltpu.CompilerParams(dimension_semantics=("arbitrary",) * n_axes, vmem_limit_bytes=vmem)


def _rms_norm(x, g):
    return (x * lax.rsqrt(jnp.mean(x * x, axis=-1, keepdims=True) + RMS_EPS)) * g


def _log_sigmoid(x):
    return -(jnp.maximum(-x, 0.0) + jnp.log1p(jnp.exp(-jnp.abs(x))))


def _in_proj_kernel(x_ref, g_ref, w_ref, bf_ref, q_ref, kf_ref, vf_ref, kb_ref, vb_ref, lf_ref, u_ref, a_ref):
    h = _rms_norm(x_ref[...], g_ref[...])
    z = jnp.dot(h.astype(BF16), w_ref[...], preferred_element_type=F32)
    a_w = ATTN_WIDTH
    q_ref[...] = (z[:, :a_w] * ATTN_SCALE).astype(BF16)
    k = z[:, a_w:2 * a_w]
    v = z[:, 2 * a_w:3 * a_w]
    kf_ref[...] = k
    vf_ref[...] = v
    kb_ref[...] = k.astype(BF16)
    vb_ref[...] = v.astype(BF16)
    u_ref[...] = z[:, _COL_U:_COL_A]
    a_ref[...] = z[:, _COL_A:_COL_G] * jax.nn.sigmoid(z[:, _COL_G:_COL_F])
    lf = _log_sigmoid(z[:, _COL_F:] + bf_ref[...])
    lf_ref[...] = lf[:, :N_HEADS]


def _in_proj(x, g, w_packed, bf_pad, tm):
    n = x.shape[0]
    row = lambda c: pl.BlockSpec((tm, c), lambda i: (i, 0))
    full = lambda a: pl.BlockSpec(a.shape, lambda i: (0,) * a.ndim)
    outs = [
        jax.ShapeDtypeStruct((n, ATTN_WIDTH), BF16),
        jax.ShapeDtypeStruct((n, ATTN_WIDTH), F32),
        jax.ShapeDtypeStruct((n, ATTN_WIDTH), F32),
        jax.ShapeDtypeStruct((n, ATTN_WIDTH), BF16),
        jax.ShapeDtypeStruct((n, ATTN_WIDTH), BF16),
        jax.ShapeDtypeStruct((n, N_HEADS), F32),
        jax.ShapeDtypeStruct((n, POOL_WIDTH), F32),
        jax.ShapeDtypeStruct((n, CONV_WIDTH), F32),
    ]
    return pl.pallas_call(
        _in_proj_kernel,
        out_shape=outs,
        grid=(n // tm,),
        in_specs=[row(D_MODEL), full(g), full(w_packed), full(bf_pad)],
        out_specs=[row(ATTN_WIDTH)] * 5 + [row(N_HEADS), row(POOL_WIDTH), row(CONV_WIDTH)],
        compiler_params=_params(1),
        name="in_proj",
    )(x, g, w_packed, bf_pad)


def _tri(n, strict):
    r = lax.broadcasted_iota(jnp.int32, (n, n), 0)
    c = lax.broadcasted_iota(jnp.int32, (n, n), 1)
    return ((r < c) if strict else (r <= c)).astype(F32)


def _chunk_total_kernel(x_ref, t_ref):
    t_ref[...] = jnp.sum(x_ref[...], axis=1, keepdims=True)


def _chunk_offset_kernel(t_ref, o_ref):
    o_ref[...] = jnp.dot(t_ref[...], _tri(t_ref.shape[1], True), preferred_element_type=F32,
                         precision=lax.Precision.HIGHEST)


def _chunk_scan_kernel(x_ref, off_ref, o_ref):
    o_ref[...] = jnp.dot(x_ref[...], _tri(LANES, False), preferred_element_type=F32,
                         precision=lax.Precision.HIGHEST) + off_ref[...]


def _cumsum_last(x):
    rows, length = x.shape
    n_chunks = length // LANES
    n = rows * n_chunks
    x2 = x.reshape(n, LANES)
    tb = min(n, 2048)
    chunk_rows = lambda c: pl.BlockSpec((tb, c), lambda i: (i, 0))
    totals = pl.pallas_call(
        _chunk_total_kernel,
        out_shape=jax.ShapeDtypeStruct((n, 1), F32),
        grid=(n // tb,), in_specs=[chunk_rows(LANES)], out_specs=chunk_rows(1),
        compiler_params=_params(1), name="cumsum_totals",
    )(x2)
    offsets = pl.pallas_call(
        _chunk_offset_kernel,
        out_shape=jax.ShapeDtypeStruct((rows, n_chunks), F32),
        name="cumsum_offsets",
    )(totals.reshape(rows, n_chunks))
    y2 = pl.pallas_call(
        _chunk_scan_kernel,
        out_shape=jax.ShapeDtypeStruct((n, LANES), F32),
        grid=(n // tb,), in_specs=[chunk_rows(LANES), chunk_rows(1)], out_specs=chunk_rows(LANES),
        compiler_params=_params(1), name="cumsum_scan",
    )(x2, offsets.reshape(n, 1))
    return y2.reshape(rows, length)


def _fox_prompt_kernel(cref_ref, q_ref, k_ref, v_ref, crow_ref, o_ref, *, tq, nq):
    b = pl.program_id(0)
    hp = pl.program_id(1)
    qi = pl.program_id(2)
    q = q_ref[...]
    lane = lax.broadcasted_iota(jnp.int32, (1, LANES), 1)
    first_head = lane < HEAD_DIM
    row = lax.broadcasted_iota(jnp.int32, (tq, tq), 0)
    col = lax.broadcasted_iota(jnp.int32, (tq, tq), 1)
    causal = col <= row
    outs = []
    for h in range(2):
        qh = jnp.where(first_head if h == 0 else jnp.logical_not(first_head), q, jnp.zeros_like(q))
        head = 2 * hp + h
        c_ref = cref_ref[(b * N_HEADS + head) * nq + qi]

        def kv_step(j, carry, masked):
            m, l, acc = carry
            off = pl.multiple_of(j * tq, tq)
            k = k_ref[pl.ds(off, tq), :]
            v = v_ref[pl.ds(off, tq), :]
            s = lax.dot_general(qh, k, (((1,), (1,)), ((), ())), preferred_element_type=F32)
            ck = crow_ref[pl.ds(head, 1), pl.ds(off, tq)]
            s = s + (c_ref - ck)
            if masked:
                s = jnp.where(causal, s, -jnp.inf)
            m_new = jnp.maximum(m, jnp.max(s, axis=1, keepdims=True))
            alpha = jnp.exp(m - m_new)
            p = jnp.exp(s - m_new)
            l = alpha * l + jnp.sum(p, axis=1, keepdims=True)
            acc = alpha * acc + jnp.dot(p.astype(BF16), v, preferred_element_type=F32)
            return m_new, l, acc

        init = (jnp.full((tq, 1), -jnp.inf, F32), jnp.zeros((tq, 1), F32), jnp.zeros((tq, LANES), F32))
        carry = lax.fori_loop(0, qi, functools.partial(kv_step, masked=False), init)
        _, l, acc = kv_step(qi, carry, True)
        outs.append(acc / l)
    o_ref[...] = jnp.where(first_head, outs[0], outs[1]).astype(BF16)


def _fox_prompt(q, kb, vb, c_row, batch, seq, tq):
    nq = seq // tq
    c_ref = c_row[:, :, tq - 1::tq].reshape(-1)
    kern = functools.partial(_fox_prompt_kernel, tq=tq, nq=nq)
    return pl.pallas_call(
        kern,
        out_shape=jax.ShapeDtypeStruct((batch * seq, ATTN_WIDTH), BF16),
        grid=(batch, N_HEADS // 2, nq),
        in_specs=[
            pl.BlockSpec(memory_space=pltpu.SMEM),
            pl.BlockSpec((tq, LANES), lambda b, hp, qi: (b * nq + qi, hp)),
            pl.BlockSpec((seq, LANES), lambda b, hp, qi: (b, hp)),
            pl.BlockSpec((seq, LANES), lambda b, hp, qi: (b, hp)),
            pl.BlockSpec((None, N_HEADS, seq), lambda b, hp, qi: (b, 0, 0)),
        ],
        out_specs=pl.BlockSpec((tq, LANES), lambda b, hp, qi: (b * nq + qi, hp)),
        compiler_params=_params(3),
        name="fox_prompt",
    )(c_ref, q, kb, vb, c_row)


def _ln_silu(y, g, b):
    mu = jnp.mean(y, axis=-1, keepdims=True)
    yc = y - mu
    var = jnp.mean(yc * yc, axis=-1, keepdims=True)
    y = yc * lax.rsqrt(var + LN_EPS) * g + b
    return y * jax.nn.sigmoid(y)


def _window_lanes(shape):
    lane = lax.broadcasted_iota(jnp.int32, shape, len(shape) - 1)
    return lane // POOL_GROUP_DIM


def _local_mix_kernel(u_ref, up_ref, a_ref, ap_ref, pw_ref, ps_ref, cw_ref, cb_ref, lg_ref, lb_ref,
                      yp_ref, yc_ref, eu_ref, ea_ref, *, ts):
    i = pl.program_id(1)
    hist = i > 0
    u = u_ref[...]
    eu_ref[0:16, :] = jnp.where(hist, up_ref[...], 0.0)
    eu_ref[16:16 + ts, :] = u
    e = eu_ref[...]
    s2 = e + pltpu.roll(e, 1, 0)
    s4 = s2 + pltpu.roll(s2, 2, 0)
    s8 = s4 + pltpu.roll(s4, 4, 0)
    s16 = s8 + pltpu.roll(s8, 8, 0)
    grp = _window_lanes((1, POOL_WIDTH))
    win = jnp.where(grp == 0, s2, jnp.where(grp == 1, s4, jnp.where(grp == 2, s8, s16)))[16:, :]
    width = jnp.where(grp == 0, 2, jnp.where(grp == 1, 4, jnp.where(grp == 2, 8, 16)))
    pos = i * ts + lax.broadcasted_iota(jnp.int32, (ts, 1), 0)
    cnt = jnp.minimum(pos + 1, width).astype(F32)
    d = win / cnt - u
    y = jnp.dot(d.astype(BF16), pw_ref[...], preferred_element_type=F32) * ps_ref[...]
    yp_ref[...] = y.astype(BF16)

    ea_ref[0:32, :] = jnp.where(hist, ap_ref[...], 0.0)
    ea_ref[32:32 + ts, :] = a_ref[...]
    acc = jnp.zeros((ts, CONV_WIDTH), F32)
    for k in range(CONV_KERNEL):
        acc = acc + ea_ref[2 + k:2 + k + ts, :] * cw_ref[k:k + 1, :]
    yc_ref[...] = _ln_silu(acc + cb_ref[...], lg_ref[...], lb_ref[...]).astype(BF16)


def _local_mix(u, a, pool_bd, pool_scale, conv_w, conv_b, ln_g, ln_b, batch, seq, ts):
    nt = seq // ts
    full = lambda arr: pl.BlockSpec(arr.shape, lambda b, i: (0,) * arr.ndim)
    cur = pl.BlockSpec((ts, POOL_WIDTH), lambda b, i: (b * nt + i, 0))
    prev = lambda rows: pl.BlockSpec(
        (rows, POOL_WIDTH), lambda b, i: (jnp.maximum((b * seq + i * ts) // rows - 1, 0), 0))
    n = batch * seq
    return pl.pallas_call(
        functools.partial(_local_mix_kernel, ts=ts),
        out_shape=[jax.ShapeDtypeStruct((n, POOL_WIDTH), BF16), jax.ShapeDtypeStruct((n, CONV_WIDTH), BF16)],
        grid=(batch, nt),
        in_specs=[cur, prev(16), cur, prev(32), full(pool_bd), full(pool_scale), full(conv_w), full(conv_b),
                  full(ln_g), full(ln_b)],
        out_specs=[cur, cur],
        scratch_shapes=[pltpu.VMEM((ts + 16, POOL_WIDTH), F32), pltpu.VMEM((ts + 32, CONV_WIDTH), F32)],
        compiler_params=_params(2),
        name="local_mix",
    )(u, u, a, a, pool_bd, pool_scale, conv_w, conv_b, ln_g, ln_b)


def _sample_mix_kernel(ep_ref, ec_ref, pw_ref, ps_ref, cw_ref, cb_ref, lg_ref, lb_ref, yp_ref, yc_ref, *, t_new):
    grp = _window_lanes((1, POOL_WIDTH))
    d_rows, c_rows = [], []
    for t in range(t_new):
        last = POOL_HIST + t
        wins = []
        for w in POOL_WINDOWS:
            tot = ep_ref[last]
            for r in range(last + 1 - w, last):
                tot = tot + ep_ref[r]
            wins.append(tot * (1.0 / w))
        mean = jnp.where(grp == 0, wins[0], jnp.where(grp == 1, wins[1], jnp.where(grp == 2, wins[2], wins[3])))
        d_rows.append(mean - ep_ref[last])
        conv = ec_ref[t] * cw_ref[0:1, :]
        for k in range(1, CONV_KERNEL):
            conv = conv + ec_ref[t + k] * cw_ref[k:k + 1, :]
        c_rows.append(conv)
    d = jnp.concatenate(d_rows, axis=0)
    y = jnp.dot(d.astype(BF16), pw_ref[...], preferred_element_type=F32) * ps_ref[...]
    yp_ref[...] = y.astype(BF16)
    c = jnp.concatenate(c_rows, axis=0) + cb_ref[...]
    yc_ref[...] = _ln_silu(c, lg_ref[...], lb_ref[...]).astype(BF16)


def _sample_mix(ext_pool, ext_conv, pool_bd, pool_scale, conv_w, conv_b, ln_g, ln_b, t_new):
    bd = ext_pool.shape[0]
    n = bd * t_new
    yp, yc = pl.pallas_call(
        functools.partial(_sample_mix_kernel, t_new=t_new),
        out_shape=[jax.ShapeDtypeStruct((n, POOL_WIDTH), BF16), jax.ShapeDtypeStruct((n, CONV_WIDTH), BF16)],
        name="sample_mix",
    )(ext_pool.transpose(1, 0, 2), ext_conv.transpose(1, 0, 2), pool_bd, pool_scale, conv_w, conv_b, ln_g, ln_b)
    reorder = lambda y: y.reshape(t_new, bd, -1).transpose(1, 0, 2).reshape(n, -1)
    return reorder(yp), reorder(yc)


def _gather_pages_kernel(pt_ref, *refs, group):
    o_ref = refs[group]
    for g in range(group):
        o_ref[g] = refs[g][...]


def _gather_logf_pages(cache_lf, page_table, group):
    bd, n_pages = page_table.shape
    _, page, heads = cache_lf.shape
    steps = n_pages // group

    def in_spec(g):
        return pl.BlockSpec((None, page, heads), lambda b, s, pt: (pt[b * n_pages + s * group + g], 0, 0))

    out = pl.pallas_call(
        functools.partial(_gather_pages_kernel, group=group),
        out_shape=jax.ShapeDtypeStruct((bd * n_pages, page, heads), F32),
        grid_spec=pltpu.PrefetchScalarGridSpec(
            num_scalar_prefetch=1,
            grid=(bd, steps),
            in_specs=[in_spec(g) for g in range(group)],
            out_specs=pl.BlockSpec((group, page, heads), lambda b, s, pt: (b * steps + s, 0, 0)),
        ),
        compiler_params=_params(2),
        name="gather_logf_pages",
    )(page_table.reshape(-1), *([cache_lf] * group))
    return out.reshape(bd, n_pages * page, heads)


def _fox_sample_kernel(pt_ref, q_ref, cq_ref, ck_ref, kn_ref, vn_ref, bn_ref, *refs, group, page, t_new):
    k_refs = refs[:group]
    v_refs = refs[group:2 * group]
    o_ref = refs[2 * group]
    qbd_ref, m_ref, l_ref, acc_ref = refs[2 * group + 1:]
    step = pl.program_id(1)
    rows = t_new * N_HEADS
    head_of_lane = lax.broadcasted_iota(jnp.int32, (N_HEADS, ATTN_WIDTH), 1) // HEAD_DIM
    head_of_row = lax.broadcasted_iota(jnp.int32, (N_HEADS, ATTN_WIDTH), 0)
    diag = head_of_lane == head_of_row

    @pl.when(step == 0)
    def _():
        q = q_ref[...].astype(F32)
        blocks = [jnp.where(diag, jnp.broadcast_to(q[t:t + 1, :], (N_HEADS, ATTN_WIDTH)), 0.0)
                  for t in range(t_new)]
        qbd_ref[...] = jnp.concatenate(blocks, axis=0).astype(BF16)
        m_ref[...] = jnp.full(m_ref.shape, -jnp.inf, F32)
        l_ref[...] = jnp.zeros(l_ref.shape, F32)
        acc_ref[...] = jnp.zeros(acc_ref.shape, F32)

    qbd = qbd_ref[...]
    nt = (((1,), (1,)), ((), ()))
    s = jnp.concatenate(
        [lax.dot_general(qbd, k_refs[g][...].astype(BF16), nt, preferred_element_type=F32) for g in range(group)],
        axis=1)
    ck = ck_ref[...]
    s = s + (cq_ref[...] - jnp.concatenate([ck] * t_new, axis=0))
    m = m_ref[...]
    m_new = jnp.maximum(m, jnp.max(s, axis=1, keepdims=True))
    alpha = jnp.exp(m - m_new)
    p = jnp.exp(s - m_new)
    l_ref[...] = alpha * l_ref[...] + jnp.sum(p, axis=1, keepdims=True)
    pb = p.astype(BF16)
    pv = jnp.dot(pb[:, :page], v_refs[0][...].astype(BF16), preferred_element_type=F32)
    for g in range(1, group):
        pv = pv + jnp.dot(pb[:, g * page:(g + 1) * page], v_refs[g][...].astype(BF16), preferred_element_type=F32)
    acc_ref[...] = alpha * acc_ref[...] + pv
    m_ref[...] = m_new

    @pl.when(step == pl.num_programs(1) - 1)
    def _():
        kn = kn_ref[...].astype(BF16)
        vn = vn_ref[...].astype(BF16).astype(F32)
        sn = lax.dot_general(qbd, kn, nt, preferred_element_type=F32) + bn_ref[...]
        m1 = m_ref[...]
        m2 = jnp.maximum(m1, jnp.max(sn, axis=1, keepdims=True))
        a2 = jnp.exp(m1 - m2)
        pn = jnp.exp(sn - m2)
        l2 = a2 * l_ref[...] + jnp.sum(pn, axis=1, keepdims=True)
        acc = a2 * acc_ref[...]
        pn = pn.astype(BF16).astype(F32)
        for t in range(t_new):
            acc = acc + pn[:, t:t + 1] * vn[t:t + 1, :]
        o_full = acc / l2
        outs = [jnp.sum(jnp.where(diag, o_full[t * N_HEADS:(t + 1) * N_HEADS, :], 0.0), axis=0, keepdims=True)
                for t in range(t_new)]
        o_ref[...] = jnp.concatenate(outs, axis=0).astype(BF16)


def _fox_sample(q, k_new, v_new, cq_rows, ck_rows, bias_new, cache_k, cache_v, page_table, group):
    bd, t_new, _ = q.shape
    n_pages = page_table.shape[1]
    n_pool, page = cache_k.shape[0], cache_k.shape[1]
    ck2 = cache_k.reshape(n_pool, page, ATTN_WIDTH)
    cv2 = cache_v.reshape(n_pool, page, ATTN_WIDTH)
    steps = n_pages // group
    rows = t_new * N_HEADS

    def page_spec(g):
        return pl.BlockSpec((None, page, ATTN_WIDTH), lambda b, s, pt: (pt[b * n_pages + s * group + g], 0, 0))

    per_b = lambda r, c: pl.BlockSpec((None, r, c), lambda b, s, pt: (b, 0, 0))
    kern = functools.partial(_fox_sample_kernel, group=group, page=page, t_new=t_new)
    return pl.pallas_call(
        kern,
        out_shape=jax.ShapeDtypeStruct((bd, t_new, ATTN_WIDTH), BF16),
        grid_spec=pltpu.PrefetchScalarGridSpec(
            num_scalar_prefetch=1,
            grid=(bd, steps),
            in_specs=[per_b(t_new, ATTN_WIDTH), per_b(rows, 1),
                      pl.BlockSpec((None, N_HEADS, group * page), lambda b, s, pt: (b, 0, s)),
                      per_b(t_new, ATTN_WIDTH), per_b(t_new, ATTN_WIDTH), per_b(rows, t_new)]
                     + [page_spec(g) for g in range(group)] * 2,
            out_specs=per_b(t_new, ATTN_WIDTH),
            scratch_shapes=[pltpu.VMEM((rows, ATTN_WIDTH), BF16), pltpu.VMEM((rows, 1), F32),
                            pltpu.VMEM((rows, 1), F32), pltpu.VMEM((rows, ATTN_WIDTH), F32)],
        ),
        compiler_params=_params(2),
        name="fox_sample",
    )(page_table.reshape(-1), q, cq_rows, ck_rows, k_new, v_new, bias_new, *([ck2] * group), *([cv2] * group))


def _top2(logits):
    lane = lax.broadcasted_iota(jnp.int32, logits.shape, 1)
    lane_f = lane.astype(F32)
    neg = jnp.float32(-jnp.inf)
    x = jnp.where(lane < N_EXPERTS, logits, neg)
    v1 = jnp.max(x, axis=1, keepdims=True)
    i1 = jnp.min(jnp.where(x == v1, lane_f, float(LANES)), axis=1, keepdims=True)
    x2 = jnp.where(lane_f == i1, neg, x)
    v2 = jnp.max(x2, axis=1, keepdims=True)
    i2 = jnp.min(jnp.where(x2 == v2, lane_f, float(LANES)), axis=1, keepdims=True)
    e2 = jnp.exp(v2 - v1)
    g1 = 1.0 / (1.0 + e2)
    g2 = e2 / (1.0 + e2)
    return i1.astype(jnp.int32), i2.astype(jnp.int32), g1, g2


def _out_proj_kernel(x_ref, ya_ref, yp_ref, yc_ref, wa_ref, wp_ref, wc_ref, *rest, route):
    acc = jnp.dot(ya_ref[...], wa_ref[...], preferred_element_type=F32)
    acc = acc + jnp.dot(yp_ref[...], wp_ref[...], preferred_element_type=F32)
    acc = acc + jnp.dot(yc_ref[...], wc_ref[...], preferred_element_type=F32)
    xn = x_ref[...] + acc
    if not route:
        (o_ref,) = rest
        o_ref[...] = xn
        return
    g_ref, wr_ref, o_ref, eid_ref, gate_ref = rest
    o_ref[...] = xn
    h = _rms_norm(xn, g_ref[...])
    logits = jnp.dot(h, wr_ref[...], preferred_element_type=F32, precision=lax.Precision.HIGHEST)
    i1, i2, g1, g2 = _top2(logits)
    lane = lax.broadcasted_iota(jnp.int32, logits.shape, 1)
    eid_ref[...] = jnp.where(lane == 0, i1, i2)[:, :2]
    gate_ref[...] = jnp.where(lane == 0, g1, g2)[:, :2]


def _out_proj(x, ya, yp, yc, wa, wp, wc, tm, router=None):
    n = x.shape[0]
    row = lambda c: pl.BlockSpec((tm, c), lambda i: (i, 0))
    full = lambda a: pl.BlockSpec(a.shape, lambda i: (0,) * a.ndim)
    ins = [x, ya, yp, yc, wa, wp, wc]
    in_specs = [row(D_MODEL), row(ATTN_WIDTH), row(POOL_WIDTH), row(CONV_WIDTH), full(wa), full(wp), full(wc)]
    outs = [jax.ShapeDtypeStruct((n, D_MODEL), F32)]
    out_specs = [row(D_MODEL)]
    if router is not None:
        ins += list(router)
        in_specs += [full(a) for a in router]
        outs += [jax.ShapeDtypeStruct((n, 2), jnp.int32), jax.ShapeDtypeStruct((n, 2), F32)]
        out_specs += [row(2), row(2)]
    return pl.pallas_call(
        functools.partial(_out_proj_kernel, route=router is not None),
        out_shape=outs,
        grid=(n // tm,),
        in_specs=in_specs,
        out_specs=out_specs,
        compiler_params=_params(1),
        name="out_proj",
    )(*ins)


def _ffn_kernel(te_ref, nv_ref, x_ref, g_ref, wg_ref, wu_ref, wd_ref, *rest, residual, final_norm, n_chunks):
    o_ref = rest[-1]
    i = pl.program_id(0)

    @pl.when(i < nv_ref[0])
    def _():
        x = x_ref[...]
        h = _rms_norm(x, g_ref[...]).astype(BF16)
        fc = wg_ref.shape[-1] // n_chunks
        acc = None
        for c in range(n_chunks):
            sl = slice(c * fc, (c + 1) * fc)
            gate = jnp.dot(h, wg_ref[:, sl], preferred_element_type=F32)
            up = jnp.dot(h, wu_ref[:, sl], preferred_element_type=F32)
            hid = (gate * jax.nn.sigmoid(gate) * up).astype(BF16)
            part = jnp.dot(hid, wd_ref[sl, :], preferred_element_type=F32)
            acc = part if acc is None else acc + part
        y = x + acc if residual else acc
        if final_norm:
            y = _rms_norm(y, rest[0][...])
        o_ref[...] = y

    @pl.when(i >= nv_ref[0])
    def _():
        o_ref[...] = jnp.zeros(o_ref.shape, F32)


def _ffn(x, g, wg, wu, wd, tile_expert, n_valid, tm, residual, final_g=None, n_chunks=2, single_buffer=False):
    n = x.shape[0]
    d_ff = wg.shape[-1]
    row = pl.BlockSpec((tm, D_MODEL), lambda i, te, nv: (i, 0))
    mode = dict(pipeline_mode=pl.Buffered(1)) if single_buffer else {}
    w_in = pl.BlockSpec((None, D_MODEL, d_ff), lambda i, te, nv: (te[i], 0, 0), **mode)
    w_out = pl.BlockSpec((None, d_ff, D_MODEL), lambda i, te, nv: (te[i], 0, 0), **mode)
    vec = pl.BlockSpec((1, D_MODEL), lambda i, te, nv: (0, 0))
    ins = [x, g, wg, wu, wd]
    in_specs = [row, vec, w_in, w_in, w_out]
    if final_g is not None:
        ins.append(final_g)
        in_specs.append(vec)
    kern = functools.partial(_ffn_kernel, residual=residual, final_norm=final_g is not None, n_chunks=n_chunks)
    return pl.pallas_call(
        kern,
        out_shape=jax.ShapeDtypeStruct((n, D_MODEL), F32),
        grid_spec=pltpu.PrefetchScalarGridSpec(
            num_scalar_prefetch=2, grid=(n // tm,), in_specs=in_specs, out_specs=row),
        compiler_params=_params(1),
        name="ffn",
    )(tile_expert, n_valid, *ins)


def _gather_rows_kernel(idx_ref, src_ref, o_ref, sem, *, tm):
    base = pl.program_id(0) * tm

    def row_copy(r, src_row):
        return pltpu.make_async_copy(src_ref.at[pl.ds(src_row, 1), :], o_ref.at[pl.ds(r, 1), :], sem)

    def start(r, _):
        row_copy(r, idx_ref[base + r]).start()
        return 0

    def wait(r, _):
        row_copy(r, 0).wait()
        return 0

    lax.fori_loop(0, tm, start, 0)
    lax.fori_loop(0, tm, wait, 0)


def _gather_rows(src, idx, tm):
    n = idx.shape[0]
    return pl.pallas_call(
        functools.partial(_gather_rows_kernel, tm=tm),
        out_shape=jax.ShapeDtypeStruct((n, D_MODEL), F32),
        grid_spec=pltpu.PrefetchScalarGridSpec(
            num_scalar_prefetch=1,
            grid=(n // tm,),
            in_specs=[pl.BlockSpec(memory_space=pl.ANY)],
            out_specs=pl.BlockSpec((tm, D_MODEL), lambda i, idx: (i, 0)),
            scratch_shapes=[pltpu.SemaphoreType.DMA],
        ),
        compiler_params=_params(1),
        name="moe_dispatch",
    )(idx, src)


def _combine_kernel(pos_ref, x_ref, gate_ref, ys_ref, *rest, tm, final_norm):
    o_ref, buf0, buf1, sem = rest[-4:]
    base = pl.program_id(0) * tm

    def row_copy(r, k, src_row):
        buf = buf0 if k == 0 else buf1
        return pltpu.make_async_copy(ys_ref.at[pl.ds(src_row, 1), :], buf.at[pl.ds(r, 1), :], sem)

    def start(r, _):
        for k in range(2):
            row_copy(r, k, pos_ref[2 * (base + r) + k]).start()
        return 0

    def wait(r, _):
        for k in range(2):
            row_copy(r, k, 0).wait()
        return 0

    lax.fori_loop(0, tm, start, 0)
    lax.fori_loop(0, tm, wait, 0)
    gate = gate_ref[...]
    moe = gate[:, 0:1] * buf0[...] + gate[:, 1:2] * buf1[...]
    y = x_ref[...] + moe
    if final_norm:
        y = _rms_norm(y, rest[0][...])
    o_ref[...] = y


def _combine(x, gates, pos, ys, tm, final_g=None):
    n = x.shape[0]
    row = lambda c: pl.BlockSpec((tm, c), lambda i, pos: (i, 0))
    ins = [x, gates, ys]
    in_specs = [row(D_MODEL), row(2), pl.BlockSpec(memory_space=pl.ANY)]
    if final_g is not None:
        ins.append(final_g)
        in_specs.append(pl.BlockSpec((1, D_MODEL), lambda i, pos: (0, 0)))
    return pl.pallas_call(
        functools.partial(_combine_kernel, tm=tm, final_norm=final_g is not None),
        out_shape=jax.ShapeDtypeStruct((n, D_MODEL), F32),
        grid_spec=pltpu.PrefetchScalarGridSpec(
            num_scalar_prefetch=1,
            grid=(n // tm,),
            in_specs=in_specs,
            out_specs=row(D_MODEL),
            scratch_shapes=[pltpu.VMEM((tm, D_MODEL), F32), pltpu.VMEM((tm, D_MODEL), F32), pltpu.SemaphoreType.DMA],
        ),
        compiler_params=_params(1),
        name="moe_combine",
    )(pos, *ins)


def _route(eid, tm):
    n_pairs = eid.shape[0] * 2
    flat = eid.reshape(-1)
    onehot = (flat[:, None] == jnp.arange(N_EXPERTS, dtype=jnp.int32)[None, :]).astype(jnp.int32)
    csum = jnp.cumsum(onehot, axis=0)
    rank = jnp.sum((csum - onehot) * onehot, axis=1)
    counts = csum[-1]
    padded = ((counts + tm - 1) // tm) * tm
    ends = jnp.cumsum(padded)
    starts = ends - padded
    pos = (starts[flat] + rank).astype(jnp.int32)
    n_tiles = -(-n_pairs // tm) + N_EXPERTS
    tile_start = jnp.arange(n_tiles, dtype=jnp.int32) * tm
    tile_expert = jnp.minimum(jnp.sum(tile_start[:, None] >= ends[None, :], axis=1), N_EXPERTS - 1).astype(jnp.int32)
    n_valid = (ends[-1] // tm).astype(jnp.int32).reshape(1)
    src = jnp.zeros((n_tiles * tm,), jnp.int32).at[pos].set(jnp.arange(n_pairs, dtype=jnp.int32) // 2)
    return pos, src, tile_expert, n_valid


def _block_diag(pool_w):
    g, c, _ = pool_w.shape
    out = jnp.zeros((g * c, g * c), pool_w.dtype)
    for i in range(g):
        out = out.at[i * c:(i + 1) * c, i * c:(i + 1) * c].set(pool_w[i])
    return out


def _pack_w_in(w_in):
    pad = jnp.zeros((D_MODEL, LANES - N_HEADS), w_in.dtype)
    o = _QKV
    wf = w_in[:, o:o + N_HEADS]
    rest = w_in[:, o + N_HEADS:]
    return jnp.concatenate([w_in[:, :o], rest, wf, pad], axis=1).astype(BF16)


def kernel(x_prompt, x_sample, cache_k, cache_v, cache_logf, state_pool, state_conv, page_table, norm_mix, w_in, b_forget, pool_w, pool_scale, conv_w, conv_b, conv_ln_g, conv_ln_b, w_out, norm_ffn, ffn_gate, ffn_up, ffn_down, moe_router, moe_gate, moe_up, moe_down, norm_final):
    batch, seq, _ = x_prompt.shape
    bd, t_new, _ = x_sample.shape
    depth = w_in.shape[0]
    n_p, n_s = batch * seq, bd * t_new
    n_pages, page = page_table.shape[1], cache_k.shape[2]
    past = n_pages * page
    tm_p, tm_s = 512, n_s
    tq = 256
    ts = 512
    page_group = 8
    tm_e = 512

    xp = x_prompt.reshape(n_p, D_MODEL)
    xs = x_sample.reshape(n_s, D_MODEL)
    row2 = lambda v: v.reshape(1, -1).astype(F32)
    new_p = [[] for _ in range(5)]
    new_s = [[] for _ in range(5)]
    for l in range(depth):
        last = l == depth - 1
        g_mix = row2(norm_mix[l])
        w_packed = _pack_w_in(w_in[l])
        bf_pad = jnp.concatenate([b_forget[l].astype(F32), jnp.zeros((LANES - N_HEADS,), F32)]).reshape(1, LANES)
        pool_bd = _block_diag(pool_w[l]).astype(BF16)
        ps, cw, cb = row2(pool_scale[l]), conv_w[l].astype(F32), row2(conv_b[l])
        lg, lb = row2(conv_ln_g[l]), row2(conv_ln_b[l])
        wo = w_out[l].astype(BF16)
        wa, wp, wc = wo[:ATTN_WIDTH], wo[ATTN_WIDTH:ATTN_WIDTH + POOL_WIDTH], wo[ATTN_WIDTH + POOL_WIDTH:]

        q, kf, vf, kb, vb, lf, u, a = _in_proj(xp, g_mix, w_packed, bf_pad, tm_p)
        c_row = _cumsum_last(lf.reshape(batch, seq, N_HEADS).transpose(0, 2, 1).reshape(batch * N_HEADS, seq))
        y_att = _fox_prompt(q, kb, vb, c_row.reshape(batch, N_HEADS, seq), batch, seq, tq)
        y_pool, y_conv = _local_mix(u, a, pool_bd, ps, cw, cb, lg, lb, batch, seq, ts)
        new_p[0].append(kf.reshape(batch, seq, N_HEADS, HEAD_DIM))
        new_p[1].append(vf.reshape(batch, seq, N_HEADS, HEAD_DIM))
        new_p[2].append(lf.reshape(batch, seq, N_HEADS))
        new_p[3].append(u.reshape(batch, seq, POOL_WIDTH)[:, seq - POOL_HIST:])
        new_p[4].append(a.reshape(batch, seq, CONV_WIDTH)[:, seq - CONV_HIST:])

        qs, kfs, vfs, _, _, lfs, us, a_s = _in_proj(xs, g_mix, w_packed, bf_pad, tm_s)
        lf_past = _gather_logf_pages(cache_logf[l], page_table, page_group)
        c_past = _cumsum_last(lf_past.transpose(0, 2, 1).reshape(bd * N_HEADS, past)).reshape(bd, N_HEADS, past)
        lf_new = lfs.reshape(bd, t_new, N_HEADS)
        c_new = c_past[:, :, -1][:, None, :] + jnp.cumsum(lf_new, axis=1)
        cq_rows = c_new.reshape(bd, t_new * N_HEADS, 1)
        tri = jnp.arange(t_new)[None, :] <= jnp.arange(t_new)[:, None]
        bias_new = c_new[:, :, None, :] - c_new[:, None, :, :]
        bias_new = jnp.where(tri[None, :, :, None], bias_new, -jnp.inf)
        bias_new = bias_new.transpose(0, 1, 3, 2).reshape(bd, t_new * N_HEADS, t_new)
        y_att_s = _fox_sample(qs.reshape(bd, t_new, ATTN_WIDTH), kfs.reshape(bd, t_new, ATTN_WIDTH),
                              vfs.reshape(bd, t_new, ATTN_WIDTH), cq_rows, c_past, bias_new,
                              cache_k[l], cache_v[l], page_table, page_group).reshape(n_s, ATTN_WIDTH)
        ext_pool = jnp.concatenate([state_pool[l].astype(F32), us.reshape(bd, t_new, POOL_WIDTH)], axis=1)
        ext_conv = jnp.concatenate([state_conv[l].astype(F32), a_s.reshape(bd, t_new, CONV_WIDTH)], axis=1)
        y_pool_s, y_conv_s = _sample_mix(ext_pool, ext_conv, pool_bd, ps, cw, cb, lg, lb, t_new)
        new_s[0].append(kfs.reshape(bd, t_new, N_HEADS, HEAD_DIM))
        new_s[1].append(vfs.reshape(bd, t_new, N_HEADS, HEAD_DIM))
        new_s[2].append(lf_new)
        new_s[3].append(ext_pool[:, t_new:])
        new_s[4].append(ext_conv[:, t_new:])

        g_ffn = row2(norm_ffn[l])
        final_g = row2(norm_final) if last else None
        i = l // 2
        if l % 2 == 0:
            (xp,) = _out_proj(xp, y_att, y_pool, y_conv, wa, wp, wc, tm_p)
            (xs,) = _out_proj(xs, y_att_s, y_pool_s, y_conv_s, wa, wp, wc, tm_s)
            wg, wu, wd = (w[i][None].astype(BF16) for w in (ffn_gate, ffn_up, ffn_down))
            dense = lambda x, tm: _ffn(x, g_ffn, wg, wu, wd, jnp.zeros((x.shape[0] // tm,), jnp.int32),
                                       jnp.full((1,), x.shape[0] // tm, jnp.int32), tm, True, final_g,
                                       single_buffer=True)
            xp, xs = dense(xp, tm_p), dense(xs, tm_s)
        else:
            wr = jnp.concatenate([moe_router[i].astype(F32), jnp.zeros((D_MODEL, LANES - N_EXPERTS), F32)], axis=1)
            xp, eid_p, gate_p = _out_proj(xp, y_att, y_pool, y_conv, wa, wp, wc, tm_p, router=(g_ffn, wr))
            xs, eid_s, gate_s = _out_proj(xs, y_att_s, y_pool_s, y_conv_s, wa, wp, wc, tm_s, router=(g_ffn, wr))
            pos, src, tile_expert, n_valid = _route(jnp.concatenate([eid_p, eid_s], axis=0), tm_e)
            x_sorted = _gather_rows(jnp.concatenate([xp, xs], axis=0), src, tm_e)
            wg, wu, wd = (w[i].astype(BF16) for w in (moe_gate, moe_up, moe_down))
            y_sorted = _ffn(x_sorted, g_ffn, wg, wu, wd, tile_expert, n_valid, tm_e, False, single_buffer=True)
            xp = _combine(xp, gate_p, pos[:2 * n_p], y_sorted, 256, final_g)
            xs = _combine(xs, gate_s, pos[2 * n_p:], y_sorted, n_s, final_g)
    y_prompt = xp.reshape(batch, seq, D_MODEL)
    y_sample = xs.reshape(bd, t_new, D_MODEL)
    k_p, v_p, lf_p, pool_p, conv_p = [jnp.stack(a, axis=0) for a in new_p]
    k_s, v_s, lf_s, pool_s, conv_s = [jnp.stack(a, axis=0) for a in new_s]
    return (y_prompt, y_sample, k_p, v_p, lf_p, pool_p, conv_p, k_s, v_s, lf_s, pool_s, conv_s)
```

```python
import functools

import jax
import jax.numpy as jnp
from jax import lax
from jax.experimental import pallas as pl
from jax.experimental.pallas import tpu as pltpu

F32 = jnp.float32
BF16 = jnp.bfloat16

D_MODEL = 1024
HEAD_DIM = 64
ATTN_WIDTH = D_MODEL // 2
N_HEADS = ATTN_WIDTH // HEAD_DIM
POOL_WIDTH = D_MODEL // 4
POOL_WINDOWS = (2, 4, 8, 16)
POOL_GROUP_DIM = POOL_WIDTH // len(POOL_WINDOWS)
CONV_WIDTH = D_MODEL - ATTN_WIDTH - POOL_WIDTH
CONV_KERNEL = 31
POOL_HIST = max(POOL_WINDOWS) - 1
CONV_HIST = CONV_KERNEL - 1
N_EXPERTS = 8
ATTN_SCALE = HEAD_DIM ** -0.5
RMS_EPS = 1e-6
LN_EPS = 1e-5

LANES = 128
VMEM_LIMIT = 56 << 20


def _params(n_axes, vmem=VMEM_LIMIT):
    return pltpu.CompilerParams(dimension_semantics=("arbitrary",) * n_axes, vmem_limit_bytes=vmem)


def _rms_norm(x, g):
    return (x * lax.rsqrt(jnp.mean(x * x, axis=-1, keepdims=True) + RMS_EPS)) * g


def _log_sigmoid(x):
    return -(jnp.maximum(-x, 0.0) + jnp.log1p(jnp.exp(-jnp.abs(x))))


def _in_proj_kernel(x_ref, g_ref, wr_ref, wt_ref, bf_ref, *rest, mix_tiles, n_aliased):
    n_in = 0 if mix_tiles is None else 6
    rest = rest[:n_in] + rest[n_in + n_aliased:]
    if mix_tiles is None:
        q_ref, vb_ref, u_ref, a_ref, kt_ref, vt_ref, ktb_ref, lft_ref, kmax_ref = rest
    else:
        (pw_ref, ps_ref, cw_ref, cb_ref, lg_ref, lb_ref, q_ref, vb_ref, u_ref, a_ref, kt_ref, vt_ref, ktb_ref,
         lft_ref, kmax_ref, yp_ref, yc_ref, eu_ref, ea_ref, sh_ref) = rest
    hb = _rms_norm(x_ref[...], g_ref[...]).astype(BF16)
    z = jnp.dot(hb, wr_ref[...], preferred_element_type=F32)
    a_w = ATTN_WIDTH
    q_ref[...] = (z[:, :a_w] * ATTN_SCALE).astype(BF16)
    vb_ref[...] = z[:, a_w:2 * a_w].astype(BF16)
    o = 2 * a_w
    u = z[:, o:o + POOL_WIDTH]
    u_ref[...] = u
    o += POOL_WIDTH
    a = z[:, o:o + CONV_WIDTH] * jax.nn.sigmoid(z[:, o + CONV_WIDTH:o + 2 * CONV_WIDTH])
    a_ref[...] = a
    if mix_tiles is not None:
        tile = pl.program_id(0) % mix_tiles
        yp, yc = _local_mix_tile(u, a, tile, pw_ref, ps_ref, cw_ref, cb_ref, lg_ref, lb_ref, eu_ref, ea_ref, sh_ref)
        yp_ref[...] = yp
        yc_ref[...] = yc
    zt = lax.dot_general(wt_ref[...], hb, (((1,), (1,)), ((), ())), preferred_element_type=F32)
    kt = zt[:a_w]
    kt_ref[...] = kt
    ktb = kt.astype(BF16)
    ktb_ref[...] = ktb
    vt_ref[...] = zt[a_w:2 * a_w]
    lft_ref[...] = _log_sigmoid(zt[2 * a_w:2 * a_w + N_HEADS] + bf_ref[...])
    ksq = ktb.astype(F32)
    ksq = (ksq * ksq).reshape(N_HEADS, HEAD_DIM, ksq.shape[-1])
    norm_max = jnp.sqrt(jnp.max(jnp.sum(ksq, axis=1), axis=1, keepdims=True))
    kmax_ref[...] = jnp.broadcast_to(norm_max, kmax_ref.shape)


def _in_proj(x, g, w_rows, w_t, bf_col, batch, seq, tm, mix_weights=None, layer_slab=None):
    n = x.shape[0]
    nt = seq // tm
    row = lambda c: pl.BlockSpec((tm, c), lambda i: (i, 0))
    full = lambda a: pl.BlockSpec(a.shape, lambda i: (0,) * a.ndim)
    if layer_slab is None:
        lead, prev = (), None
        col = lambda r: pl.BlockSpec((None, r, tm), lambda i: (i // nt, 0, i % nt))
        state_col = col
    else:
        layer, depth, prev = layer_slab
        lead = (depth,)
        col = lambda r: pl.BlockSpec((None, r, tm), lambda i: (i // nt, 0, i % nt))
        state_col = lambda r: pl.BlockSpec((None, None, r, tm), lambda i: (layer, i // nt, 0, i % nt))
    aliased = list(prev) if prev is not None else []
    mix = mix_weights is not None
    extra_in = list(mix_weights) if mix else []
    extra_out = [jax.ShapeDtypeStruct((n, POOL_WIDTH), BF16), jax.ShapeDtypeStruct((n, CONV_WIDTH), BF16)] if mix else []
    scratch = [pltpu.VMEM((tm + 16, POOL_WIDTH), F32), pltpu.VMEM((tm + 32, CONV_WIDTH), F32),
               pltpu.VMEM((7, tm + 32, CONV_WIDTH), F32)] if mix else []
    outs = [
        jax.ShapeDtypeStruct((n, ATTN_WIDTH), BF16),
        jax.ShapeDtypeStruct((n, ATTN_WIDTH), BF16),
        jax.ShapeDtypeStruct((n, POOL_WIDTH), F32),
        jax.ShapeDtypeStruct((n, CONV_WIDTH), F32),
        jax.ShapeDtypeStruct(lead + (batch, ATTN_WIDTH, seq), F32),
        jax.ShapeDtypeStruct(lead + (batch, ATTN_WIDTH, seq), F32),
        jax.ShapeDtypeStruct((batch, ATTN_WIDTH, seq), BF16),
        jax.ShapeDtypeStruct(lead + (batch, N_HEADS, seq), F32),
        jax.ShapeDtypeStruct((n // tm, N_HEADS, LANES), F32),
    ]
    n_in = 5 + len(extra_in)
    return pl.pallas_call(
        functools.partial(_in_proj_kernel, mix_tiles=nt if mix else None, n_aliased=len(aliased)),
        out_shape=outs + extra_out,
        grid=(n // tm,),
        in_specs=[row(D_MODEL), full(g), full(w_rows), full(w_t), full(bf_col)] + [full(w) for w in extra_in]
                 + [pl.BlockSpec(memory_space=pl.ANY)] * len(aliased),
        out_specs=[row(ATTN_WIDTH), row(ATTN_WIDTH), row(POOL_WIDTH), row(CONV_WIDTH),
                   state_col(ATTN_WIDTH), state_col(ATTN_WIDTH), col(ATTN_WIDTH), state_col(N_HEADS),
                   pl.BlockSpec((None, N_HEADS, LANES), lambda i: (i, 0, 0))]
                  + ([row(POOL_WIDTH), row(CONV_WIDTH)] if mix else []),
        scratch_shapes=scratch,
        input_output_aliases={n_in: 4, n_in + 1: 5, n_in + 2: 7} if aliased else {},
        compiler_params=_params(1),
        name="in_proj",
    )(x, g, w_rows, w_t, bf_col, *extra_in, *aliased)


def _scan_matrix(n, kind):
    t = lax.broadcasted_iota(jnp.int32, (n, n), 0)
    i = lax.broadcasted_iota(jnp.int32, (n, n), 1)
    if kind == "prefix":
        return (t <= i).astype(F32)
    if kind == "before":
        return (t < i).astype(F32)
    return -(t > i).astype(F32)


def _chunk_total_kernel(x_ref, t_ref):
    t_ref[...] = jnp.sum(x_ref[...], axis=1, keepdims=True)


def _chunk_offset_kernel(t_ref, o_ref, *, kind):
    o_ref[...] = jnp.dot(t_ref[...], _scan_matrix(t_ref.shape[1], kind), preferred_element_type=F32,
                         precision=lax.Precision.HIGHEST)


def _chunk_scan_kernel(x_ref, off_ref, o_ref, *, kind):
    o_ref[...] = jnp.dot(x_ref[...], _scan_matrix(LANES, kind), preferred_element_type=F32,
                         precision=lax.Precision.HIGHEST) + off_ref[...]


def _cumsum_last(x, neg_suffix=False):
    inner, outer = ("neg_after", "neg_after") if neg_suffix else ("prefix", "before")
    rows, length = x.shape
    n_chunks = length // LANES
    n = rows * n_chunks
    x2 = x.reshape(n, LANES)
    tb = min(n, 2048)
    chunk_rows = lambda c: pl.BlockSpec((tb, c), lambda i: (i, 0))
    totals = pl.pallas_call(
        _chunk_total_kernel,
        out_shape=jax.ShapeDtypeStruct((n, 1), F32),
        grid=(n // tb,), in_specs=[chunk_rows(LANES)], out_specs=chunk_rows(1),
        compiler_params=_params(1), name="cumsum_totals",
    )(x2)
    offsets = pl.pallas_call(
        functools.partial(_chunk_offset_kernel, kind=outer),
        out_shape=jax.ShapeDtypeStruct((rows, n_chunks), F32),
        name="cumsum_offsets",
    )(totals.reshape(rows, n_chunks))
    y2 = pl.pallas_call(
        functools.partial(_chunk_scan_kernel, kind=inner),
        out_shape=jax.ShapeDtypeStruct((n, LANES), F32),
        grid=(n // tb,), in_specs=[chunk_rows(LANES), chunk_rows(1)], out_specs=chunk_rows(LANES),
        compiler_params=_params(1), name="cumsum_scan",
    )(x2, offsets.reshape(n, 1))
    return y2.reshape(rows, length)


EXP_ZERO_BELOW = 105.0


def _fox_prompt_kernel(cref_ref, cblk_ref, kmax_ref, q_ref, kt_ref, v_ref, crow_ref, o_ref, *, tq, tk, nq):
    b = pl.program_id(0)
    hp = pl.program_id(1)
    qi = pl.program_id(2)
    q = q_ref[...]
    lane = lax.broadcasted_iota(jnp.int32, (1, LANES), 1)
    first_head = lane < HEAD_DIM
    zero = jnp.zeros_like(q)
    q2 = jnp.concatenate([jnp.where(first_head, q, zero), jnp.where(first_head, zero, q)], axis=0)
    row = lax.broadcasted_iota(jnp.int32, (tq, tk), 0)
    col = lax.broadcasted_iota(jnp.int32, (tq, tk), 1)
    base = (b * N_HEADS + 2 * hp) * nq + qi
    c_last = (cref_ref[base], cref_ref[base + nq])
    per_q = tq // tk

    def scores(j, diag):
        off = pl.multiple_of(j * tk, tk)
        s = jnp.dot(q2, kt_ref[:, pl.ds(off, tk)], preferred_element_type=F32)
        halves = []
        for h in range(2):
            sh = s[h * tq:(h + 1) * tq] + (c_last[h] - crow_ref[pl.ds(2 * hp + h, 1), pl.ds(off, tk)])
            if diag is not None:
                sh = jnp.where(col + diag * tk <= row, sh, -jnp.inf)
            halves.append(sh)
        return jnp.concatenate(halves, axis=0), v_ref[pl.ds(off, tk), :]

    def kv_step(j, carry, diag=None):
        m_old, l_old, acc_old = carry
        s, v = scores(j, diag)
        m_new = jnp.maximum(m_old, jnp.max(s, axis=1, keepdims=True))
        alpha = jnp.exp(m_old - m_new)
        p = jnp.exp(s - m_new)
        l_new = alpha * l_old + jnp.sum(p, axis=1, keepdims=True)
        acc_new = alpha * acc_old + jnp.dot(p.astype(BF16), v, preferred_element_type=F32)
        return m_new, l_new, acc_new

    carry = (jnp.full((2 * tq, 1), -jnp.inf, F32), jnp.zeros((2 * tq, 1), F32), jnp.zeros((2 * tq, LANES), F32))
    for d in range(per_q):
        carry = kv_step(qi * per_q + d, carry, diag=d)

    m_floor = jnp.min(carry[0])
    qsq = q.astype(F32) * q.astype(F32)
    q_norm = [jnp.sqrt(jnp.max(jnp.sum(jnp.where(first_head if h == 0 else jnp.logical_not(first_head), qsq, 0.0),
                                       axis=1, keepdims=True))) for h in range(2)]

    def count_skippable(j, n):
        bound = None
        for h in range(2):
            head = b * N_HEADS + 2 * hp + h
            bh = q_norm[h] * kmax_ref[head] + (c_last[h] - cblk_ref[head * nq * per_q + j])
            bound = bh if bound is None else jnp.maximum(bound, bh)
        skippable = jnp.logical_and(n == j, bound - m_floor <= -EXP_ZERO_BELOW)
        return n + skippable.astype(jnp.int32)

    n_skip = lax.fori_loop(0, qi * per_q, count_skippable, jnp.int32(0))
    carry = lax.fori_loop(n_skip, qi * per_q, kv_step, carry)
    _, l, acc = carry
    out = acc / l
    o_ref[...] = jnp.where(first_head, out[:tq], out[tq:]).astype(BF16)


def _fox_prompt(q, ktb, vb, c_row, k_norm_max, batch, seq, tq, tk):
    nq = seq // tq
    c_ref = c_row[:, :, tq - 1::tq].reshape(-1)
    c_blk = c_row[:, :, tk - 1::tk].reshape(-1)
    kern = functools.partial(_fox_prompt_kernel, tq=tq, tk=tk, nq=nq)
    smem = pl.BlockSpec(memory_space=pltpu.SMEM)
    return pl.pallas_call(
        kern,
        out_shape=jax.ShapeDtypeStruct((batch * seq, ATTN_WIDTH), BF16),
        grid=(batch, N_HEADS // 2, nq),
        in_specs=[
            smem, smem, smem,
            pl.BlockSpec((tq, LANES), lambda b, hp, qi: (b * nq + qi, hp)),
            pl.BlockSpec((None, LANES, seq), lambda b, hp, qi: (b, hp, 0)),
            pl.BlockSpec((seq, LANES), lambda b, hp, qi: (b, hp)),
            pl.BlockSpec((None, N_HEADS, seq), lambda b, hp, qi: (b, 0, 0)),
        ],
        out_specs=pl.BlockSpec((tq, LANES), lambda b, hp, qi: (b * nq + qi, hp)),
        compiler_params=_params(3),
        name="fox_prompt",
    )(c_ref, c_blk, k_norm_max.reshape(-1), q, ktb, vb, c_row)


def _ln_silu(y, g, b):
    mu = jnp.mean(y, axis=-1, keepdims=True)
    yc = y - mu
    var = jnp.mean(yc * yc, axis=-1, keepdims=True)
    y = yc * lax.rsqrt(var + LN_EPS) * g + b
    return y * jax.nn.sigmoid(y)


def _window_lanes(shape):
    lane = lax.broadcasted_iota(jnp.int32, shape, len(shape) - 1)
    return lane // POOL_GROUP_DIM


def _local_mix_tile(u, a, tile, pw_ref, ps_ref, cw_ref, cb_ref, lg_ref, lb_ref, eu_ref, ea_ref, sh_ref):
    ts = u.shape[0]

    @pl.when(tile == 0)
    def _():
        eu_ref[0:16, :] = jnp.zeros((16, POOL_WIDTH), F32)
        ea_ref[0:32, :] = jnp.zeros((32, CONV_WIDTH), F32)

    @pl.when(tile > 0)
    def _():
        eu_ref[0:16, :] = eu_ref[ts:ts + 16, :]
        ea_ref[0:32, :] = ea_ref[ts:ts + 32, :]

    eu_ref[16:16 + ts, :] = u
    ea_ref[32:32 + ts, :] = a
    e = eu_ref[...]
    s2 = e + pltpu.roll(e, 1, 0)
    s4 = s2 + pltpu.roll(s2, 2, 0)
    s8 = s4 + pltpu.roll(s4, 4, 0)
    s16 = s8 + pltpu.roll(s8, 8, 0)
    grp = _window_lanes((1, POOL_WIDTH))
    win = jnp.where(grp == 0, s2, jnp.where(grp == 1, s4, jnp.where(grp == 2, s8, s16)))[16:, :]
    width = jnp.where(grp == 0, 2, jnp.where(grp == 1, 4, jnp.where(grp == 2, 8, 16)))
    pos = tile * ts + lax.broadcasted_iota(jnp.int32, (ts, 1), 0)
    cnt = jnp.minimum(pos + 1, width).astype(F32)
    d = win / cnt - u
    y_pool = jnp.dot(d.astype(BF16), pw_ref[...], preferred_element_type=F32) * ps_ref[...]

    acc = jnp.zeros((ts, CONV_WIDTH), F32)
    for r in range(8):
        taps = [k for k in range(CONV_KERNEL) if (2 + k) % 8 == r]
        lo = min(2 + k for k in taps) - r
        hi = max(2 + k for k in taps) - r
        if r == 0:
            shifted, base = ea_ref, lo
        else:
            sh_ref[r - 1, 0:hi - lo + ts, :] = ea_ref[lo + r:hi + r + ts, :]
            shifted, base = sh_ref.at[r - 1], 0
        for k in taps:
            start = base + 2 + k - r - lo
            acc = acc + shifted[start:start + ts, :] * cw_ref[k:k + 1, :]
    y_conv = _ln_silu(acc + cb_ref[...], lg_ref[...], lb_ref[...])
    return y_pool.astype(BF16), y_conv.astype(BF16)


def _sample_mix_kernel(ep_ref, ec_ref, pw_ref, ps_ref, cw_ref, cb_ref, lg_ref, lb_ref, yp_ref, yc_ref, *, t_new):
    grp = _window_lanes((1, POOL_WIDTH))
    d_rows, c_rows = [], []
    for t in range(t_new):
        last = POOL_HIST + t
        wins = []
        for w in POOL_WINDOWS:
            tot = ep_ref[last]
            for r in range(last + 1 - w, last):
                tot = tot + ep_ref[r]
            wins.append(tot * (1.0 / w))
        mean = jnp.where(grp == 0, wins[0], jnp.where(grp == 1, wins[1], jnp.where(grp == 2, wins[2], wins[3])))
        d_rows.append(mean - ep_ref[last])
        conv = ec_ref[t] * cw_ref[0:1, :]
        for k in range(1, CONV_KERNEL):
            conv = conv + ec_ref[t + k] * cw_ref[k:k + 1, :]
        c_rows.append(conv)
    d = jnp.concatenate(d_rows, axis=0)
    y = jnp.dot(d.astype(BF16), pw_ref[...], preferred_element_type=F32) * ps_ref[...]
    yp_ref[...] = y.astype(BF16)
    c = jnp.concatenate(c_rows, axis=0) + cb_ref[...]
    yc_ref[...] = _ln_silu(c, lg_ref[...], lb_ref[...]).astype(BF16)


def _sample_mix(ext_pool, ext_conv, pool_bd, pool_scale, conv_w, conv_b, ln_g, ln_b, t_new):
    bd = ext_pool.shape[0]
    n = bd * t_new
    yp, yc = pl.pallas_call(
        functools.partial(_sample_mix_kernel, t_new=t_new),
        out_shape=[jax.ShapeDtypeStruct((n, POOL_WIDTH), BF16), jax.ShapeDtypeStruct((n, CONV_WIDTH), BF16)],
        name="sample_mix",
    )(ext_pool.transpose(1, 0, 2), ext_conv.transpose(1, 0, 2), pool_bd, pool_scale, conv_w, conv_b, ln_g, ln_b)
    reorder = lambda y: y.reshape(t_new, bd, -1).transpose(1, 0, 2).reshape(n, -1)
    return reorder(yp), reorder(yc)


def _gather_pages_kernel(pt_ref, *refs, group, page):
    o_ref = refs[group]
    for g in range(group):
        o_ref[:, g * page:(g + 1) * page] = refs[g][...]


def _gather_logf_pages(cache_lf_t, layer, page_table, group):
    bd, n_pages = page_table.shape
    heads, page = cache_lf_t.shape[2], cache_lf_t.shape[3]
    steps = n_pages // group

    def in_spec(g):
        return pl.BlockSpec((None, None, heads, page),
                            lambda b, s, pt: (layer, pt[b * n_pages + s * group + g], 0, 0))

    return pl.pallas_call(
        functools.partial(_gather_pages_kernel, group=group, page=page),
        out_shape=jax.ShapeDtypeStruct((bd, heads, n_pages * page), F32),
        grid_spec=pltpu.PrefetchScalarGridSpec(
            num_scalar_prefetch=1,
            grid=(bd, steps),
            in_specs=[in_spec(g) for g in range(group)],
            out_specs=pl.BlockSpec((None, heads, group * page), lambda b, s, pt: (b, 0, s)),
        ),
        compiler_params=_params(2),
        name="gather_logf_pages",
    )(page_table.reshape(-1), *([cache_lf_t] * group))


def _fox_sample_kernel(pt_ref, q_ref, cq_ref, ck_ref, kn_ref, vn_ref, bn_ref, *refs, group, page, t_new):
    kt_refs = refs[:group]
    vt_refs = refs[group:2 * group]
    o_ref = refs[2 * group]
    qbd_ref, m_ref, l_ref, acc_ref = refs[2 * group + 1:]
    step = pl.program_id(1)
    head_of_lane = lax.broadcasted_iota(jnp.int32, (N_HEADS, ATTN_WIDTH), 1) // HEAD_DIM
    head_of_row = lax.broadcasted_iota(jnp.int32, (N_HEADS, ATTN_WIDTH), 0)
    diag = head_of_lane == head_of_row

    @pl.when(step == 0)
    def _():
        q = q_ref[...].astype(F32)
        blocks = [jnp.where(diag, jnp.broadcast_to(q[t:t + 1, :], (N_HEADS, ATTN_WIDTH)), 0.0)
                  for t in range(t_new)]
        qbd_ref[...] = jnp.concatenate(blocks, axis=0).astype(BF16)
        m_ref[...] = jnp.full(m_ref.shape, -jnp.inf, F32)
        l_ref[...] = jnp.zeros(l_ref.shape, F32)
        acc_ref[...] = jnp.zeros(acc_ref.shape, F32)

    qbd = qbd_ref[...]
    nt = (((1,), (1,)), ((), ()))
    s = jnp.concatenate(
        [jnp.dot(qbd, kt_refs[g][...].astype(BF16), preferred_element_type=F32) for g in range(group)], axis=1)
    ck = ck_ref[...]
    s = s + (cq_ref[...] - jnp.concatenate([ck] * t_new, axis=0))
    m = m_ref[...]
    m_new = jnp.maximum(m, jnp.max(s, axis=1, keepdims=True))
    alpha = jnp.exp(m - m_new)
    p = jnp.exp(s - m_new)
    l_ref[...] = alpha * l_ref[...] + jnp.sum(p, axis=1, keepdims=True)
    pb = p.astype(BF16)
    pv = lax.dot_general(pb[:, :page], vt_refs[0][...].astype(BF16), nt, preferred_element_type=F32)
    for g in range(1, group):
        pv = pv + lax.dot_general(pb[:, g * page:(g + 1) * page], vt_refs[g][...].astype(BF16), nt,
                                  preferred_element_type=F32)
    acc_ref[...] = alpha * acc_ref[...] + pv
    m_ref[...] = m_new

    @pl.when(step == pl.num_programs(1) - 1)
    def _():
        kn = kn_ref[...].astype(BF16)
        vn = vn_ref[...].astype(BF16).astype(F32)
        sn = lax.dot_general(qbd, kn, nt, preferred_element_type=F32) + bn_ref[...]
        m1 = m_ref[...]
        m2 = jnp.maximum(m1, jnp.max(sn, axis=1, keepdims=True))
        a2 = jnp.exp(m1 - m2)
        pn = jnp.exp(sn - m2)
        l2 = a2 * l_ref[...] + jnp.sum(pn, axis=1, keepdims=True)
        acc = a2 * acc_ref[...]
        pn = pn.astype(BF16).astype(F32)
        for t in range(t_new):
            acc = acc + pn[:, t:t + 1] * vn[t:t + 1, :]
        o_full = acc / l2
        outs = [jnp.sum(jnp.where(diag, o_full[t * N_HEADS:(t + 1) * N_HEADS, :], 0.0), axis=0, keepdims=True)
                for t in range(t_new)]
        o_ref[...] = jnp.concatenate(outs, axis=0).astype(BF16)


def _fox_sample(q, k_new, v_new, cq_rows, ck_rows, bias_new, cache_kt, cache_vt, layer, page_table, group):
    bd, t_new, _ = q.shape
    n_pages = page_table.shape[1]
    page = cache_kt.shape[-1]
    steps = n_pages // group
    rows = t_new * N_HEADS

    def page_spec(g):
        return pl.BlockSpec((None, None, ATTN_WIDTH, page),
                            lambda b, s, pt: (layer, pt[b * n_pages + s * group + g], 0, 0))

    per_b = lambda r, c: pl.BlockSpec((None, r, c), lambda b, s, pt: (b, 0, 0))
    kern = functools.partial(_fox_sample_kernel, group=group, page=page, t_new=t_new)
    return pl.pallas_call(
        kern,
        out_shape=jax.ShapeDtypeStruct((bd, t_new, ATTN_WIDTH), BF16),
        grid_spec=pltpu.PrefetchScalarGridSpec(
            num_scalar_prefetch=1,
            grid=(bd, steps),
            in_specs=[per_b(t_new, ATTN_WIDTH), per_b(rows, 1),
                      pl.BlockSpec((None, N_HEADS, group * page), lambda b, s, pt: (b, 0, s)),
                      per_b(t_new, ATTN_WIDTH), per_b(t_new, ATTN_WIDTH), per_b(rows, t_new)]
                     + [page_spec(g) for g in range(group)] * 2,
            out_specs=per_b(t_new, ATTN_WIDTH),
            scratch_shapes=[pltpu.VMEM((rows, ATTN_WIDTH), BF16), pltpu.VMEM((rows, 1), F32),
                            pltpu.VMEM((rows, 1), F32), pltpu.VMEM((rows, ATTN_WIDTH), F32)],
        ),
        compiler_params=_params(2),
        name="fox_sample",
    )(page_table.reshape(-1), q, cq_rows, ck_rows, k_new, v_new, bias_new,
      *([cache_kt] * group), *([cache_vt] * group))


def _top2(logits):
    lane = lax.broadcasted_iota(jnp.int32, logits.shape, 1)
    lane_f = lane.astype(F32)
    neg = jnp.float32(-jnp.inf)
    x = jnp.where(lane < N_EXPERTS, logits, neg)
    v1 = jnp.max(x, axis=1, keepdims=True)
    i1 = jnp.min(jnp.where(x == v1, lane_f, float(LANES)), axis=1, keepdims=True)
    x2 = jnp.where(lane_f == i1, neg, x)
    v2 = jnp.max(x2, axis=1, keepdims=True)
    i2 = jnp.min(jnp.where(x2 == v2, lane_f, float(LANES)), axis=1, keepdims=True)
    e2 = jnp.exp(v2 - v1)
    g1 = 1.0 / (1.0 + e2)
    g2 = e2 / (1.0 + e2)
    return i1.astype(jnp.int32), i2.astype(jnp.int32), g1, g2


def _out_proj_kernel(x_ref, ya_ref, yp_ref, yc_ref, wa_ref, wp_ref, wc_ref, *rest, route):
    acc = jnp.dot(ya_ref[...], wa_ref[...], preferred_element_type=F32)
    acc = acc + jnp.dot(yp_ref[...], wp_ref[...], preferred_element_type=F32)
    acc = acc + jnp.dot(yc_ref[...], wc_ref[...], preferred_element_type=F32)
    xn = x_ref[...] + acc
    if not route:
        (o_ref,) = rest
        o_ref[...] = xn
        return
    g_ref, wr_ref, o_ref, eid_ref, gate_ref = rest
    o_ref[...] = xn
    h = _rms_norm(xn, g_ref[...])
    h_hi = h.astype(BF16)
    h_lo = (h - h_hi.astype(F32)).astype(BF16)
    w_hi, w_lo = wr_ref[0], wr_ref[1]
    logits = (jnp.dot(h_hi, w_hi, preferred_element_type=F32) + jnp.dot(h_hi, w_lo, preferred_element_type=F32)
              + jnp.dot(h_lo, w_hi, preferred_element_type=F32))
    i1, i2, g1, g2 = _top2(logits)
    lane = lax.broadcasted_iota(jnp.int32, logits.shape, 1)
    eid_ref[...] = jnp.where(lane == 0, i1, i2)[:, :2]
    gate_ref[...] = jnp.where(lane == 0, g1, g2)[:, :2]


def _out_proj(x, ya, yp, yc, wa, wp, wc, tm, router=None):
    n = x.shape[0]
    row = lambda c: pl.BlockSpec((tm, c), lambda i: (i, 0))
    full = lambda a: pl.BlockSpec(a.shape, lambda i: (0,) * a.ndim)
    ins = [x, ya, yp, yc, wa, wp, wc]
    in_specs = [row(D_MODEL), row(ATTN_WIDTH), row(POOL_WIDTH), row(CONV_WIDTH), full(wa), full(wp), full(wc)]
    outs = [jax.ShapeDtypeStruct((n, D_MODEL), F32)]
    out_specs = [row(D_MODEL)]
    if router is not None:
        ins += list(router)
        in_specs += [full(a) for a in router]
        outs += [jax.ShapeDtypeStruct((n, 2), jnp.int32), jax.ShapeDtypeStruct((n, 2), F32)]
        out_specs += [row(2), row(2)]
    return pl.pallas_call(
        functools.partial(_out_proj_kernel, route=router is not None),
        out_shape=outs,
        grid=(n // tm,),
        in_specs=in_specs,
        out_specs=out_specs,
        compiler_params=_params(1),
        name="out_proj",
    )(*ins)


FFN_CHUNKS = 2


def _swiglu(x, g, wg_ref, wu_ref, wd_ref):
    h = _rms_norm(x, g).astype(BF16)
    fc = wg_ref.shape[-1] // FFN_CHUNKS
    acc = None
    for c in range(FFN_CHUNKS):
        sl = slice(c * fc, (c + 1) * fc)
        gate = jnp.dot(h, wg_ref[:, sl], preferred_element_type=F32)
        up = jnp.dot(h, wu_ref[:, sl], preferred_element_type=F32)
        hid = (gate * jax.nn.sigmoid(gate) * up).astype(BF16)
        part = jnp.dot(hid, wd_ref[sl, :], preferred_element_type=F32)
        acc = part if acc is None else acc + part
    return acc


def _ffn_dense_kernel(x_ref, g_ref, wg_ref, wu_ref, wd_ref, *rest, final_norm):
    o_ref = rest[-1]
    x = x_ref[...]
    y = x + _swiglu(x, g_ref[...], wg_ref, wu_ref, wd_ref)
    if final_norm:
        y = _rms_norm(y, rest[0][...])
    o_ref[...] = y


def _ffn_dense(x, g, wg, wu, wd, tm, final_g=None):
    n = x.shape[0]
    row = pl.BlockSpec((tm, D_MODEL), lambda i: (i, 0))
    resident = lambda a: pl.BlockSpec(a.shape, lambda i: (0,) * a.ndim, pipeline_mode=pl.Buffered(1))
    vec = pl.BlockSpec((1, D_MODEL), lambda i: (0, 0))
    ins = [x, g, wg, wu, wd]
    in_specs = [row, vec, resident(wg), resident(wu), resident(wd)]
    if final_g is not None:
        ins.append(final_g)
        in_specs.append(vec)
    return pl.pallas_call(
        functools.partial(_ffn_dense_kernel, final_norm=final_g is not None),
        out_shape=jax.ShapeDtypeStruct((n, D_MODEL), F32),
        grid=(n // tm,),
        in_specs=in_specs,
        out_specs=row,
        compiler_params=_params(1),
        name="ffn_dense",
    )(*ins)


def _ffn_experts_kernel(src_ref, te_ref, nv_ref, x_hbm, g_ref, wg_ref, wu_ref, wd_ref, o_ref, xbuf, sems, *, tm):
    i = pl.program_id(0)
    n_valid = nv_ref[0]

    def row_copy(tile, r, src_row):
        slot = tile % 2
        return pltpu.make_async_copy(x_hbm.at[pl.ds(src_row, 1), :], xbuf.at[slot, pl.ds(r, 1), :], sems.at[slot])

    def gather(tile):
        for r in range(tm):
            row_copy(tile, r, src_ref[tile * tm + r]).start()

    @pl.when(jnp.logical_and(i == 0, n_valid > 0))
    def _():
        gather(0)

    @pl.when(i + 1 < n_valid)
    def _():
        gather(i + 1)

    @pl.when(i < n_valid)
    def _():
        def wait(r, _):
            row_copy(i, r, 0).wait()
            return 0
        lax.fori_loop(0, tm, wait, 0, unroll=8)
        o_ref[...] = _swiglu(xbuf[i % 2], g_ref[...], wg_ref, wu_ref, wd_ref)

    @pl.when(i >= n_valid)
    def _():
        o_ref[...] = jnp.zeros(o_ref.shape, F32)


def _ffn_experts(x_all, src, tile_expert, n_valid, g, wg, wu, wd, tm):
    n_slots = src.shape[0]
    d_ff = wg.shape[-1]
    row = pl.BlockSpec((tm, D_MODEL), lambda i, src, te, nv: (i, 0))
    w_in = pl.BlockSpec((None, D_MODEL, d_ff), lambda i, src, te, nv: (te[i], 0, 0), pipeline_mode=pl.Buffered(1))
    w_out = pl.BlockSpec((None, d_ff, D_MODEL), lambda i, src, te, nv: (te[i], 0, 0), pipeline_mode=pl.Buffered(1))
    vec = pl.BlockSpec((1, D_MODEL), lambda i, src, te, nv: (0, 0))
    return pl.pallas_call(
        functools.partial(_ffn_experts_kernel, tm=tm),
        out_shape=jax.ShapeDtypeStruct((n_slots, D_MODEL), F32),
        grid_spec=pltpu.PrefetchScalarGridSpec(
            num_scalar_prefetch=3,
            grid=(n_slots // tm,),
            in_specs=[pl.BlockSpec(memory_space=pl.ANY), vec, w_in, w_in, w_out],
            out_specs=row,
            scratch_shapes=[pltpu.VMEM((2, tm, D_MODEL), F32), pltpu.SemaphoreType.DMA((2,))],
        ),
        compiler_params=_params(1),
        name="ffn_experts",
    )(src, tile_expert, n_valid, x_all, g, wg, wu, wd)


def _combine_kernel(pos_ref, x_ref, gate_ref, ys_ref, *rest, tm, final_norm):
    o_ref, buf, sems = rest[-3:]
    i = pl.program_id(0)

    def row_copy(tile, r, k, src_row):
        slot = tile % 2
        return pltpu.make_async_copy(ys_ref.at[pl.ds(src_row, 1), :], buf.at[slot, k, pl.ds(r, 1), :],
                                     sems.at[slot])

    def gather(tile):
        for r in range(tm):
            for k in range(2):
                row_copy(tile, r, k, pos_ref[2 * tile * tm + 2 * r + k]).start()

    @pl.when(i == 0)
    def _():
        gather(0)

    @pl.when(i + 1 < pl.num_programs(0))
    def _():
        gather(i + 1)

    def wait(r, _):
        for k in range(2):
            row_copy(i, r, k, 0).wait()
        return 0

    lax.fori_loop(0, tm, wait, 0, unroll=4)
    gate = gate_ref[...]
    slot = i % 2
    moe = gate[:, 0:1] * buf[slot, 0] + gate[:, 1:2] * buf[slot, 1]
    y = x_ref[...] + moe
    if final_norm:
        y = _rms_norm(y, rest[0][...])
    o_ref[...] = y


def _combine(x, gates, pos, ys, tm, final_g=None):
    n = x.shape[0]
    row = lambda c: pl.BlockSpec((tm, c), lambda i, pos: (i, 0))
    ins = [x, gates, ys]
    in_specs = [row(D_MODEL), row(2), pl.BlockSpec(memory_space=pl.ANY)]
    if final_g is not None:
        ins.append(final_g)
        in_specs.append(pl.BlockSpec((1, D_MODEL), lambda i, pos: (0, 0)))
    return pl.pallas_call(
        functools.partial(_combine_kernel, tm=tm, final_norm=final_g is not None),
        out_shape=jax.ShapeDtypeStruct((n, D_MODEL), F32),
        grid_spec=pltpu.PrefetchScalarGridSpec(
            num_scalar_prefetch=1,
            grid=(n // tm,),
            in_specs=in_specs,
            out_specs=row(D_MODEL),
            scratch_shapes=[pltpu.VMEM((2, 2, tm, D_MODEL), F32), pltpu.SemaphoreType.DMA((2,))],
        ),
        compiler_params=_params(1),
        name="moe_combine",
    )(pos, *ins)


def _route(eid, tm):
    n_pairs = eid.shape[0] * 2
    flat = eid.reshape(-1)
    onehot = (flat[:, None] == jnp.arange(N_EXPERTS, dtype=jnp.int32)[None, :]).astype(jnp.int32)
    csum = jnp.cumsum(onehot, axis=0)
    rank = jnp.sum((csum - onehot) * onehot, axis=1)
    counts = csum[-1]
    padded = ((counts + tm - 1) // tm) * tm
    ends = jnp.cumsum(padded)
    starts = ends - padded
    pos = (starts[flat] + rank).astype(jnp.int32)
    n_tiles = -(-n_pairs // tm) + N_EXPERTS
    tile_start = jnp.arange(n_tiles, dtype=jnp.int32) * tm
    tile_expert = jnp.minimum(jnp.sum(tile_start[:, None] >= ends[None, :], axis=1), N_EXPERTS - 1).astype(jnp.int32)
    n_valid = (ends[-1] // tm).astype(jnp.int32).reshape(1)
    src = jnp.zeros((n_tiles * tm,), jnp.int32).at[pos].set(jnp.arange(n_pairs, dtype=jnp.int32) // 2)
    return pos, src, tile_expert, n_valid


def _block_diag(pool_w):
    g, c, _ = pool_w.shape
    out = jnp.zeros((g * c, g * c), pool_w.dtype)
    for i in range(g):
        out = out.at[i * c:(i + 1) * c, i * c:(i + 1) * c].set(pool_w[i])
    return out


def _split_w_in(w_in):
    a_w = ATTN_WIDTH
    q, k, v = w_in[:, :a_w], w_in[:, a_w:2 * a_w], w_in[:, 2 * a_w:3 * a_w]
    f = w_in[:, 3 * a_w:3 * a_w + N_HEADS]
    rest = w_in[:, 3 * a_w + N_HEADS:]
    w_rows = jnp.concatenate([q, v, rest], axis=1).astype(BF16)
    pad = jnp.zeros((D_MODEL, 16 - N_HEADS), w_in.dtype)
    w_t = jnp.concatenate([k, v, f, pad], axis=1).T.astype(BF16)
    return w_rows, w_t


def kernel(x_prompt, x_sample, cache_k, cache_v, cache_logf, state_pool, state_conv, page_table, norm_mix, w_in, b_forget, pool_w, pool_scale, conv_w, conv_b, conv_ln_g, conv_ln_b, w_out, norm_ffn, ffn_gate, ffn_up, ffn_down, moe_router, moe_gate, moe_up, moe_down, norm_final):
    batch, seq, _ = x_prompt.shape
    bd, t_new, _ = x_sample.shape
    depth = w_in.shape[0]
    n_p, n_s = batch * seq, bd * t_new
    n_pages, page = page_table.shape[1], cache_k.shape[2]
    past = n_pages * page
    tm_p, tm_s = 512, n_s
    tq, tk = 512, 512
    ts = 512
    page_group = 16
    tm_e = 512
    assert tq % tk == 0 and seq % tq == 0 and seq % tm_p == 0 and n_pages % page_group == 0

    xp = x_prompt.reshape(n_p, D_MODEL)
    xs = x_sample.reshape(n_s, D_MODEL)
    row2 = lambda v: v.reshape(1, -1).astype(F32)
    n_pool = cache_k.shape[1]
    cache_kt = cache_k.transpose(0, 1, 3, 4, 2).reshape(depth, n_pool, ATTN_WIDTH, page)
    cache_vt = cache_v.transpose(0, 1, 3, 4, 2).reshape(depth, n_pool, ATTN_WIDTH, page)
    cache_lft = cache_logf.transpose(0, 1, 3, 2)
    new_p = [[] for _ in range(5)]
    state_slabs = None
    new_s = [[] for _ in range(5)]
    for l in range(depth):
        last = l == depth - 1
        g_mix = row2(norm_mix[l])
        w_rows, w_t = _split_w_in(w_in[l])
        bf_col = b_forget[l].astype(F32).reshape(N_HEADS, 1)
        pool_bd = _block_diag(pool_w[l]).astype(BF16)
        ps, cw, cb = row2(pool_scale[l]), conv_w[l].astype(F32), row2(conv_b[l])
        lg, lb = row2(conv_ln_g[l]), row2(conv_ln_b[l])
        wo = w_out[l].astype(BF16)
        wa, wp, wc = wo[:ATTN_WIDTH], wo[ATTN_WIDTH:ATTN_WIDTH + POOL_WIDTH], wo[ATTN_WIDTH + POOL_WIDTH:]

        q, vb, u, a, kt_all, vt_all, ktb, lft_all, kmax, y_pool, y_conv = _in_proj(
            xp, g_mix, w_rows, w_t, bf_col, batch, seq, tm_p, mix_weights=(pool_bd, ps, cw, cb, lg, lb),
            layer_slab=(l, depth, state_slabs))
        state_slabs = (kt_all, vt_all, lft_all)
        c_row = _cumsum_last(lft_all[l].reshape(batch * N_HEADS, seq)).reshape(batch, N_HEADS, seq)
        k_norm_max = jnp.max(kmax[:, :, 0].reshape(batch, seq // tm_p, N_HEADS), axis=1)
        y_att = _fox_prompt(q, ktb, vb, c_row, k_norm_max, batch, seq, tq, tk)
        new_p[3].append(u.reshape(batch, seq, POOL_WIDTH)[:, seq - POOL_HIST:])
        new_p[4].append(a.reshape(batch, seq, CONV_WIDTH)[:, seq - CONV_HIST:])

        qs, _, us, a_s, kts, vts, _, lfts, _ = _in_proj(xs, g_mix, w_rows, w_t, bf_col, 1, n_s, tm_s)
        kfs, vfs, lfs = kts[0].T, vts[0].T, lfts[0].T
        lf_past = _gather_logf_pages(cache_lft, l, page_table, page_group)
        c_past = _cumsum_last(lf_past.reshape(bd * N_HEADS, past), neg_suffix=True).reshape(bd, N_HEADS, past)
        lf_new = lfs.reshape(bd, t_new, N_HEADS)
        c_new = jnp.cumsum(lf_new, axis=1)
        cq_rows = c_new.reshape(bd, t_new * N_HEADS, 1)
        tri = jnp.arange(t_new)[None, :] <= jnp.arange(t_new)[:, None]
        bias_new = c_new[:, :, None, :] - c_new[:, None, :, :]
        bias_new = jnp.where(tri[None, :, :, None], bias_new, -jnp.inf)
        bias_new = bias_new.transpose(0, 1, 3, 2).reshape(bd, t_new * N_HEADS, t_new)
        y_att_s = _fox_sample(qs.reshape(bd, t_new, ATTN_WIDTH), kfs.reshape(bd, t_new, ATTN_WIDTH),
                              vfs.reshape(bd, t_new, ATTN_WIDTH), cq_rows, c_past, bias_new,
                              cache_kt, cache_vt, l, page_table, page_group).reshape(n_s, ATTN_WIDTH)
        ext_pool = jnp.concatenate([state_pool[l].astype(F32), us.reshape(bd, t_new, POOL_WIDTH)], axis=1)
        ext_conv = jnp.concatenate([state_conv[l].astype(F32), a_s.reshape(bd, t_new, CONV_WIDTH)], axis=1)
        y_pool_s, y_conv_s = _sample_mix(ext_pool, ext_conv, pool_bd, ps, cw, cb, lg, lb, t_new)
        new_s[0].append(kfs.reshape(bd, t_new, N_HEADS, HEAD_DIM))
        new_s[1].append(vfs.reshape(bd, t_new, N_HEADS, HEAD_DIM))
        new_s[2].append(lf_new)
        new_s[3].append(ext_pool[:, t_new:])
        new_s[4].append(ext_conv[:, t_new:])

        g_ffn = row2(norm_ffn[l])
        final_g = row2(norm_final) if last else None
        i = l // 2
        if l % 2 == 0:
            (xp,) = _out_proj(xp, y_att, y_pool, y_conv, wa, wp, wc, tm_p)
            (xs,) = _out_proj(xs, y_att_s, y_pool_s, y_conv_s, wa, wp, wc, tm_s)
            wg, wu, wd = (w[i].astype(BF16) for w in (ffn_gate, ffn_up, ffn_down))
            xp = _ffn_dense(xp, g_ffn, wg, wu, wd, tm_p, final_g)
            xs = _ffn_dense(xs, g_ffn, wg, wu, wd, tm_s, final_g)
        else:
            wr = jnp.concatenate([moe_router[i].astype(F32), jnp.zeros((D_MODEL, LANES - N_EXPERTS), F32)], axis=1)
            wr_hi = wr.astype(BF16)
            wr = jnp.stack([wr_hi, (wr - wr_hi.astype(F32)).astype(BF16)])
            xp, eid_p, gate_p = _out_proj(xp, y_att, y_pool, y_conv, wa, wp, wc, tm_p, router=(g_ffn, wr))
            xs, eid_s, gate_s = _out_proj(xs, y_att_s, y_pool_s, y_conv_s, wa, wp, wc, tm_s, router=(g_ffn, wr))
            pos, src, tile_expert, n_valid = _route(jnp.concatenate([eid_p, eid_s], axis=0), tm_e)
            wg, wu, wd = (w[i].astype(BF16) for w in (moe_gate, moe_up, moe_down))
            y_sorted = _ffn_experts(jnp.concatenate([xp, xs], axis=0), src, tile_expert, n_valid,
                                    g_ffn, wg, wu, wd, tm_e)
            xp = _combine(xp, gate_p, pos[:2 * n_p], y_sorted, 256, final_g)
            xs = _combine(xs, gate_s, pos[2 * n_p:], y_sorted, n_s, final_g)
    y_prompt = xp.reshape(batch, seq, D_MODEL)
    y_sample = xs.reshape(bd, t_new, D_MODEL)
    kt_p, vt_p, lft_p = state_slabs
    pool_p, conv_p = jnp.stack(new_p[3], axis=0), jnp.stack(new_p[4], axis=0)
    heads_last = lambda t: t.reshape(depth, batch, N_HEADS, HEAD_DIM, seq).transpose(0, 1, 4, 2, 3)
    k_p, v_p, lf_p = heads_last(kt_p), heads_last(vt_p), lft_p.transpose(0, 1, 3, 2)
    k_s, v_s, lf_s, pool_s, conv_s = [jnp.stack(a, axis=0) for a in new_s]
    return (y_prompt, y_sample, k_p, v_p, lf_p, pool_p, conv_p, k_s, v_s, lf_s, pool_s, conv_s)
```

```python
import functools

import jax
import jax.numpy as jnp
from jax import lax
from jax.experimental import pallas as pl
from jax.experimental.pallas import tpu as pltpu

F32 = jnp.float32
BF16 = jnp.bfloat16

D_MODEL = 1024
HEAD_DIM = 64
ATTN_WIDTH = D_MODEL // 2
N_HEADS = ATTN_WIDTH // HEAD_DIM
POOL_WIDTH = D_MODEL // 4
POOL_WINDOWS = (2, 4, 8, 16)
POOL_GROUP_DIM = POOL_WIDTH // len(POOL_WINDOWS)
CONV_WIDTH = D_MODEL - ATTN_WIDTH - POOL_WIDTH
CONV_KERNEL = 31
POOL_HIST = max(POOL_WINDOWS) - 1
CONV_HIST = CONV_KERNEL - 1
N_EXPERTS = 8
ATTN_SCALE = HEAD_DIM ** -0.5
RMS_EPS = 1e-6
LN_EPS = 1e-5

LANES = 128
VMEM_LIMIT = 56 << 20


def _params(n_axes, vmem=VMEM_LIMIT):
    return pltpu.CompilerParams(dimension_semantics=("arbitrary",) * n_axes, vmem_limit_bytes=vmem)


def _rms_norm(x, g):
    return (x * lax.rsqrt(jnp.mean(x * x, axis=-1, keepdims=True) + RMS_EPS)) * g


def _log_sigmoid(x):
    return -(jnp.maximum(-x, 0.0) + jnp.log1p(jnp.exp(-jnp.abs(x))))


def _in_proj_kernel(x_ref, g_ref, wr_ref, wt_ref, bf_ref, *rest, mix_tiles, n_prev):
    n_in = 0 if mix_tiles is None else 6
    prev_refs = rest[n_in:n_in + 3 * n_prev]
    rest = rest[:n_in] + rest[n_in + 3 * n_prev:]
    if mix_tiles is None:
        q_ref, vb_ref, u_ref, a_ref, kt_ref, vt_ref, ktb_ref, lft_ref, kmax_ref = rest
    else:
        (pw_ref, ps_ref, cw_ref, cb_ref, lg_ref, lb_ref, q_ref, vb_ref, u_ref, a_ref, kt_ref, vt_ref, ktb_ref,
         lft_ref, kmax_ref, yp_ref, yc_ref, eu_ref, ea_ref, sh_ref) = rest
    hb = _rms_norm(x_ref[...], g_ref[...]).astype(BF16)
    z = jnp.dot(hb, wr_ref[...], preferred_element_type=F32)
    a_w = ATTN_WIDTH
    q_ref[...] = (z[:, :a_w] * ATTN_SCALE).astype(BF16)
    vb_ref[...] = z[:, a_w:2 * a_w].astype(BF16)
    o = 2 * a_w
    u = z[:, o:o + POOL_WIDTH]
    u_ref[...] = u
    o += POOL_WIDTH
    a = z[:, o:o + CONV_WIDTH] * jax.nn.sigmoid(z[:, o + CONV_WIDTH:o + 2 * CONV_WIDTH])
    a_ref[...] = a
    if mix_tiles is not None:
        tile = pl.program_id(0) % mix_tiles
        yp, yc = _local_mix_tile(u, a, tile, pw_ref, ps_ref, cw_ref, cb_ref, lg_ref, lb_ref, eu_ref, ea_ref, sh_ref)
        yp_ref[...] = yp
        yc_ref[...] = yc
    zt = lax.dot_general(wt_ref[...], hb, (((1,), (1,)), ((), ())), preferred_element_type=F32)
    kt = zt[:a_w]
    ktb = kt.astype(BF16)
    ktb_ref[...] = ktb
    state = (kt, zt[a_w:2 * a_w], _log_sigmoid(zt[2 * a_w:2 * a_w + N_HEADS] + bf_ref[...]))
    for s, (o_ref, new) in enumerate(zip((kt_ref, vt_ref, lft_ref), state)):
        if n_prev == 0:
            o_ref[...] = new
        else:
            for j in range(n_prev):
                o_ref[j] = prev_refs[s * n_prev + j][...]
            o_ref[n_prev] = new
    ksq = ktb.astype(F32)
    ksq = (ksq * ksq).reshape(N_HEADS, HEAD_DIM, ksq.shape[-1])
    norm_max = jnp.sqrt(jnp.max(jnp.sum(ksq, axis=1), axis=1, keepdims=True))
    kmax_ref[...] = jnp.broadcast_to(norm_max, kmax_ref.shape)


def _in_proj(x, g, w_rows, w_t, bf_col, batch, seq, tm, mix_weights=None, earlier=None):
    n = x.shape[0]
    nt = seq // tm
    row = lambda c: pl.BlockSpec((tm, c), lambda i: (i, 0))
    col = lambda r: pl.BlockSpec((None, r, tm), lambda i: (i // nt, 0, i % nt))
    full = lambda a: pl.BlockSpec(a.shape, lambda i: (0,) * a.ndim)
    n_prev = len(earlier[0]) if earlier is not None else 0
    prev_in = [a for group in earlier for a in group] if n_prev else []
    prev_specs = [col(a.shape[1]) for a in prev_in]
    lead = (n_prev + 1,) if n_prev else ()
    state_col = (lambda r: pl.BlockSpec((n_prev + 1, None, r, tm), lambda i: (0, i // nt, 0, i % nt))) if n_prev else col
    mix = mix_weights is not None
    extra_in = list(mix_weights) if mix else []
    extra_out = [jax.ShapeDtypeStruct((n, POOL_WIDTH), BF16), jax.ShapeDtypeStruct((n, CONV_WIDTH), BF16)] if mix else []
    scratch = [pltpu.VMEM((tm + 16, POOL_WIDTH), F32), pltpu.VMEM((tm + 32, CONV_WIDTH), F32),
               pltpu.VMEM((7, tm + 32, CONV_WIDTH), F32)] if mix else []
    outs = [
        jax.ShapeDtypeStruct((n, ATTN_WIDTH), BF16),
        jax.ShapeDtypeStruct((n, ATTN_WIDTH), BF16),
        jax.ShapeDtypeStruct((n, POOL_WIDTH), F32),
        jax.ShapeDtypeStruct((n, CONV_WIDTH), F32),
        jax.ShapeDtypeStruct(lead + (batch, ATTN_WIDTH, seq), F32),
        jax.ShapeDtypeStruct(lead + (batch, ATTN_WIDTH, seq), F32),
        jax.ShapeDtypeStruct((batch, ATTN_WIDTH, seq), BF16),
        jax.ShapeDtypeStruct(lead + (batch, N_HEADS, seq), F32),
        jax.ShapeDtypeStruct((n // tm, N_HEADS, LANES), F32),
    ]
    return pl.pallas_call(
        functools.partial(_in_proj_kernel, mix_tiles=nt if mix else None, n_prev=n_prev),
        out_shape=outs + extra_out,
        grid=(n // tm,),
        in_specs=[row(D_MODEL), full(g), full(w_rows), full(w_t), full(bf_col)] + [full(w) for w in extra_in]
                 + prev_specs,
        out_specs=[row(ATTN_WIDTH), row(ATTN_WIDTH), row(POOL_WIDTH), row(CONV_WIDTH),
                   state_col(ATTN_WIDTH), state_col(ATTN_WIDTH), col(ATTN_WIDTH), state_col(N_HEADS),
                   pl.BlockSpec((None, N_HEADS, LANES), lambda i: (i, 0, 0))]
                  + ([row(POOL_WIDTH), row(CONV_WIDTH)] if mix else []),
        scratch_shapes=scratch,
        compiler_params=_params(1),
        name="in_proj",
    )(x, g, w_rows, w_t, bf_col, *extra_in, *prev_in)


def _scan_matrix(n, kind):
    t = lax.broadcasted_iota(jnp.int32, (n, n), 0)
    i = lax.broadcasted_iota(jnp.int32, (n, n), 1)
    if kind == "prefix":
        return (t <= i).astype(F32)
    if kind == "before":
        return (t < i).astype(F32)
    return -(t > i).astype(F32)


def _chunk_total_kernel(x_ref, t_ref):
    t_ref[...] = jnp.sum(x_ref[...], axis=1, keepdims=True)


def _chunk_offset_kernel(t_ref, o_ref, *, kind):
    o_ref[...] = jnp.dot(t_ref[...], _scan_matrix(t_ref.shape[1], kind), preferred_element_type=F32,
                         precision=lax.Precision.HIGHEST)


def _chunk_scan_kernel(x_ref, off_ref, o_ref, *, kind):
    o_ref[...] = jnp.dot(x_ref[...], _scan_matrix(LANES, kind), preferred_element_type=F32,
                         precision=lax.Precision.HIGHEST) + off_ref[...]


def _cumsum_last(x, neg_suffix=False):
    inner, outer = ("neg_after", "neg_after") if neg_suffix else ("prefix", "before")
    rows, length = x.shape
    n_chunks = length // LANES
    n = rows * n_chunks
    x2 = x.reshape(n, LANES)
    tb = min(n, 2048)
    chunk_rows = lambda c: pl.BlockSpec((tb, c), lambda i: (i, 0))
    totals = pl.pallas_call(
        _chunk_total_kernel,
        out_shape=jax.ShapeDtypeStruct((n, 1), F32),
        grid=(n // tb,), in_specs=[chunk_rows(LANES)], out_specs=chunk_rows(1),
        compiler_params=_params(1), name="cumsum_totals",
    )(x2)
    offsets = pl.pallas_call(
        functools.partial(_chunk_offset_kernel, kind=outer),
        out_shape=jax.ShapeDtypeStruct((rows, n_chunks), F32),
        name="cumsum_offsets",
    )(totals.reshape(rows, n_chunks))
    y2 = pl.pallas_call(
        functools.partial(_chunk_scan_kernel, kind=inner),
        out_shape=jax.ShapeDtypeStruct((n, LANES), F32),
        grid=(n // tb,), in_specs=[chunk_rows(LANES), chunk_rows(1)], out_specs=chunk_rows(LANES),
        compiler_params=_params(1), name="cumsum_scan",
    )(x2, offsets.reshape(n, 1))
    return y2.reshape(rows, length)


EXP_ZERO_BELOW = 105.0


def _fox_prompt_kernel(cref_ref, cblk_ref, kmax_ref, q_ref, kt_ref, v_ref, crow_ref, o_ref, *, tq, tk, nq):
    b = pl.program_id(0)
    hp = pl.program_id(1)
    qi = pl.program_id(2)
    q = q_ref[...]
    lane = lax.broadcasted_iota(jnp.int32, (1, LANES), 1)
    first_head = lane < HEAD_DIM
    zero = jnp.zeros_like(q)
    q2 = jnp.concatenate([jnp.where(first_head, q, zero), jnp.where(first_head, zero, q)], axis=0)
    row = lax.broadcasted_iota(jnp.int32, (tq, tk), 0)
    col = lax.broadcasted_iota(jnp.int32, (tq, tk), 1)
    base = (b * N_HEADS + 2 * hp) * nq + qi
    c_last = (cref_ref[base], cref_ref[base + nq])
    per_q = tq // tk

    def scores(j, diag):
        off = pl.multiple_of(j * tk, tk)
        s = jnp.dot(q2, kt_ref[:, pl.ds(off, tk)], preferred_element_type=F32)
        halves = []
        for h in range(2):
            sh = s[h * tq:(h + 1) * tq] + (c_last[h] - crow_ref[pl.ds(2 * hp + h, 1), pl.ds(off, tk)])
            if diag is not None:
                sh = jnp.where(col + diag * tk <= row, sh, -jnp.inf)
            halves.append(sh)
        return jnp.concatenate(halves, axis=0), v_ref[pl.ds(off, tk), :]

    def kv_step(j, carry, diag=None):
        m_old, l_old, acc_old = carry
        s, v = scores(j, diag)
        m_new = jnp.maximum(m_old, jnp.max(s, axis=1, keepdims=True))
        alpha = jnp.exp(m_old - m_new)
        p = jnp.exp(s - m_new)
        l_new = alpha * l_old + jnp.sum(p, axis=1, keepdims=True)
        acc_new = alpha * acc_old + jnp.dot(p.astype(BF16), v, preferred_element_type=F32)
        return m_new, l_new, acc_new

    carry = (jnp.full((2 * tq, 1), -jnp.inf, F32), jnp.zeros((2 * tq, 1), F32), jnp.zeros((2 * tq, LANES), F32))
    for d in range(per_q):
        carry = kv_step(qi * per_q + d, carry, diag=d)

    m_floor = jnp.min(carry[0])
    qsq = q.astype(F32) * q.astype(F32)
    q_norm = [jnp.sqrt(jnp.max(jnp.sum(jnp.where(first_head if h == 0 else jnp.logical_not(first_head), qsq, 0.0),
                                       axis=1, keepdims=True))) for h in range(2)]

    def count_skippable(j, n):
        bound = None
        for h in range(2):
            head = b * N_HEADS + 2 * hp + h
            bh = q_norm[h] * kmax_ref[head] + (c_last[h] - cblk_ref[head * nq * per_q + j])
            bound = bh if bound is None else jnp.maximum(bound, bh)
        skippable = jnp.logical_and(n == j, bound - m_floor <= -EXP_ZERO_BELOW)
        return n + skippable.astype(jnp.int32)

    n_skip = lax.fori_loop(0, qi * per_q, count_skippable, jnp.int32(0))
    carry = lax.fori_loop(n_skip, qi * per_q, kv_step, carry)
    _, l, acc = carry
    out = acc / l
    o_ref[...] = jnp.where(first_head, out[:tq], out[tq:]).astype(BF16)


def _fox_prompt(q, ktb, vb, c_row, k_norm_max, batch, seq, tq, tk):
    nq = seq // tq
    c_ref = c_row[:, :, tq - 1::tq].reshape(-1)
    c_blk = c_row[:, :, tk - 1::tk].reshape(-1)
    kern = functools.partial(_fox_prompt_kernel, tq=tq, tk=tk, nq=nq)
    smem = pl.BlockSpec(memory_space=pltpu.SMEM)
    return pl.pallas_call(
        kern,
        out_shape=jax.ShapeDtypeStruct((batch * seq, ATTN_WIDTH), BF16),
        grid=(batch, N_HEADS // 2, nq),
        in_specs=[
            smem, smem, smem,
            pl.BlockSpec((tq, LANES), lambda b, hp, qi: (b * nq + qi, hp)),
            pl.BlockSpec((None, LANES, seq), lambda b, hp, qi: (b, hp, 0)),
            pl.BlockSpec((seq, LANES), lambda b, hp, qi: (b, hp)),
            pl.BlockSpec((None, N_HEADS, seq), lambda b, hp, qi: (b, 0, 0)),
        ],
        out_specs=pl.BlockSpec((tq, LANES), lambda b, hp, qi: (b * nq + qi, hp)),
        compiler_params=_params(3),
        name="fox_prompt",
    )(c_ref, c_blk, k_norm_max.reshape(-1), q, ktb, vb, c_row)


def _ln_silu(y, g, b):
    mu = jnp.mean(y, axis=-1, keepdims=True)
    yc = y - mu
    var = jnp.mean(yc * yc, axis=-1, keepdims=True)
    y = yc * lax.rsqrt(var + LN_EPS) * g + b
    return y * jax.nn.sigmoid(y)


def _window_lanes(shape):
    lane = lax.broadcasted_iota(jnp.int32, shape, len(shape) - 1)
    return lane // POOL_GROUP_DIM


def _local_mix_tile(u, a, tile, pw_ref, ps_ref, cw_ref, cb_ref, lg_ref, lb_ref, eu_ref, ea_ref, sh_ref):
    ts = u.shape[0]

    @pl.when(tile == 0)
    def _():
        eu_ref[0:16, :] = jnp.zeros((16, POOL_WIDTH), F32)
        ea_ref[0:32, :] = jnp.zeros((32, CONV_WIDTH), F32)

    @pl.when(tile > 0)
    def _():
        eu_ref[0:16, :] = eu_ref[ts:ts + 16, :]
        ea_ref[0:32, :] = ea_ref[ts:ts + 32, :]

    eu_ref[16:16 + ts, :] = u
    ea_ref[32:32 + ts, :] = a
    e = eu_ref[...]
    s2 = e + pltpu.roll(e, 1, 0)
    s4 = s2 + pltpu.roll(s2, 2, 0)
    s8 = s4 + pltpu.roll(s4, 4, 0)
    s16 = s8 + pltpu.roll(s8, 8, 0)
    grp = _window_lanes((1, POOL_WIDTH))
    win = jnp.where(grp == 0, s2, jnp.where(grp == 1, s4, jnp.where(grp == 2, s8, s16)))[16:, :]
    width = jnp.where(grp == 0, 2, jnp.where(grp == 1, 4, jnp.where(grp == 2, 8, 16)))
    pos = tile * ts + lax.broadcasted_iota(jnp.int32, (ts, 1), 0)
    cnt = jnp.minimum(pos + 1, width).astype(F32)
    d = win / cnt - u
    y_pool = jnp.dot(d.astype(BF16), pw_ref[...], preferred_element_type=F32) * ps_ref[...]

    acc = jnp.zeros((ts, CONV_WIDTH), F32)
    for r in range(8):
        taps = [k for k in range(CONV_KERNEL) if (2 + k) % 8 == r]
        lo = min(2 + k for k in taps) - r
        hi = max(2 + k for k in taps) - r
        if r == 0:
            shifted, base = ea_ref, lo
        else:
            sh_ref[r - 1, 0:hi - lo + ts, :] = ea_ref[lo + r:hi + r + ts, :]
            shifted, base = sh_ref.at[r - 1], 0
        for k in taps:
            start = base + 2 + k - r - lo
            acc = acc + shifted[start:start + ts, :] * cw_ref[k:k + 1, :]
    y_conv = _ln_silu(acc + cb_ref[...], lg_ref[...], lb_ref[...])
    return y_pool.astype(BF16), y_conv.astype(BF16)


def _sample_mix_kernel(ep_ref, ec_ref, pw_ref, ps_ref, cw_ref, cb_ref, lg_ref, lb_ref, yp_ref, yc_ref, *, t_new):
    grp = _window_lanes((1, POOL_WIDTH))
    d_rows, c_rows = [], []
    for t in range(t_new):
        last = POOL_HIST + t
        wins = []
        for w in POOL_WINDOWS:
            tot = ep_ref[last]
            for r in range(last + 1 - w, last):
                tot = tot + ep_ref[r]
            wins.append(tot * (1.0 / w))
        mean = jnp.where(grp == 0, wins[0], jnp.where(grp == 1, wins[1], jnp.where(grp == 2, wins[2], wins[3])))
        d_rows.append(mean - ep_ref[last])
        conv = ec_ref[t] * cw_ref[0:1, :]
        for k in range(1, CONV_KERNEL):
            conv = conv + ec_ref[t + k] * cw_ref[k:k + 1, :]
        c_rows.append(conv)
    d = jnp.concatenate(d_rows, axis=0)
    y = jnp.dot(d.astype(BF16), pw_ref[...], preferred_element_type=F32) * ps_ref[...]
    yp_ref[...] = y.astype(BF16)
    c = jnp.concatenate(c_rows, axis=0) + cb_ref[...]
    yc_ref[...] = _ln_silu(c, lg_ref[...], lb_ref[...]).astype(BF16)


def _sample_mix(ext_pool, ext_conv, pool_bd, pool_scale, conv_w, conv_b, ln_g, ln_b, t_new):
    bd = ext_pool.shape[0]
    n = bd * t_new
    yp, yc = pl.pallas_call(
        functools.partial(_sample_mix_kernel, t_new=t_new),
        out_shape=[jax.ShapeDtypeStruct((n, POOL_WIDTH), BF16), jax.ShapeDtypeStruct((n, CONV_WIDTH), BF16)],
        name="sample_mix",
    )(ext_pool.transpose(1, 0, 2), ext_conv.transpose(1, 0, 2), pool_bd, pool_scale, conv_w, conv_b, ln_g, ln_b)
    reorder = lambda y: y.reshape(t_new, bd, -1).transpose(1, 0, 2).reshape(n, -1)
    return reorder(yp), reorder(yc)


def _gather_pages_kernel(pt_ref, *refs, group, page):
    o_ref = refs[group]
    for g in range(group):
        o_ref[:, g * page:(g + 1) * page] = refs[g][...]


def _gather_logf_pages(cache_lf_t, layer, page_table, group):
    bd, n_pages = page_table.shape
    heads, page = cache_lf_t.shape[2], cache_lf_t.shape[3]
    steps = n_pages // group

    def in_spec(g):
        return pl.BlockSpec((None, None, heads, page),
                            lambda b, s, pt: (layer, pt[b * n_pages + s * group + g], 0, 0))

    return pl.pallas_call(
        functools.partial(_gather_pages_kernel, group=group, page=page),
        out_shape=jax.ShapeDtypeStruct((bd, heads, n_pages * page), F32),
        grid_spec=pltpu.PrefetchScalarGridSpec(
            num_scalar_prefetch=1,
            grid=(bd, steps),
            in_specs=[in_spec(g) for g in range(group)],
            out_specs=pl.BlockSpec((None, heads, group * page), lambda b, s, pt: (b, 0, s)),
        ),
        compiler_params=_params(2),
        name="gather_logf_pages",
    )(page_table.reshape(-1), *([cache_lf_t] * group))


def _fox_sample_kernel(pt_ref, q_ref, cq_ref, ck_ref, kn_ref, vn_ref, bn_ref, *refs, group, page, t_new):
    kt_refs = refs[:group]
    vt_refs = refs[group:2 * group]
    o_ref = refs[2 * group]
    qbd_ref, m_ref, l_ref, acc_ref = refs[2 * group + 1:]
    step = pl.program_id(1)
    head_of_lane = lax.broadcasted_iota(jnp.int32, (N_HEADS, ATTN_WIDTH), 1) // HEAD_DIM
    head_of_row = lax.broadcasted_iota(jnp.int32, (N_HEADS, ATTN_WIDTH), 0)
    diag = head_of_lane == head_of_row

    @pl.when(step == 0)
    def _():
        q = q_ref[...].astype(F32)
        blocks = [jnp.where(diag, jnp.broadcast_to(q[t:t + 1, :], (N_HEADS, ATTN_WIDTH)), 0.0)
                  for t in range(t_new)]
        qbd_ref[...] = jnp.concatenate(blocks, axis=0).astype(BF16)
        m_ref[...] = jnp.full(m_ref.shape, -jnp.inf, F32)
        l_ref[...] = jnp.zeros(l_ref.shape, F32)
        acc_ref[...] = jnp.zeros(acc_ref.shape, F32)

    qbd = qbd_ref[...]
    nt = (((1,), (1,)), ((), ()))
    s = jnp.concatenate(
        [jnp.dot(qbd, kt_refs[g][...].astype(BF16), preferred_element_type=F32) for g in range(group)], axis=1)
    ck = ck_ref[...]
    s = s + (cq_ref[...] - jnp.concatenate([ck] * t_new, axis=0))
    m = m_ref[...]
    m_new = jnp.maximum(m, jnp.max(s, axis=1, keepdims=True))
    alpha = jnp.exp(m - m_new)
    p = jnp.exp(s - m_new)
    l_ref[...] = alpha * l_ref[...] + jnp.sum(p, axis=1, keepdims=True)
    pb = p.astype(BF16)
    pv = lax.dot_general(pb[:, :page], vt_refs[0][...].astype(BF16), nt, preferred_element_type=F32)
    for g in range(1, group):
        pv = pv + lax.dot_general(pb[:, g * page:(g + 1) * page], vt_refs[g][...].astype(BF16), nt,
                                  preferred_element_type=F32)
    acc_ref[...] = alpha * acc_ref[...] + pv
    m_ref[...] = m_new

    @pl.when(step == pl.num_programs(1) - 1)
    def _():
        kn = kn_ref[...].astype(BF16)
        vn = vn_ref[...].astype(BF16).astype(F32)
        sn = lax.dot_general(qbd, kn, nt, preferred_element_type=F32) + bn_ref[...]
        m1 = m_ref[...]
        m2 = jnp.maximum(m1, jnp.max(sn, axis=1, keepdims=True))
        a2 = jnp.exp(m1 - m2)
        pn = jnp.exp(sn - m2)
        l2 = a2 * l_ref[...] + jnp.sum(pn, axis=1, keepdims=True)
        acc = a2 * acc_ref[...]
        pn = pn.astype(BF16).astype(F32)
        for t in range(t_new):
            acc = acc + pn[:, t:t + 1] * vn[t:t + 1, :]
        o_full = acc / l2
        outs = [jnp.sum(jnp.where(diag, o_full[t * N_HEADS:(t + 1) * N_HEADS, :], 0.0), axis=0, keepdims=True)
                for t in range(t_new)]
        o_ref[...] = jnp.concatenate(outs, axis=0).astype(BF16)


def _fox_sample(q, k_new, v_new, cq_rows, ck_rows, bias_new, cache_kt, cache_vt, layer, page_table, group):
    bd, t_new, _ = q.shape
    n_pages = page_table.shape[1]
    page = cache_kt.shape[-1]
    steps = n_pages // group
    rows = t_new * N_HEADS

    def page_spec(g):
        return pl.BlockSpec((None, None, ATTN_WIDTH, page),
                            lambda b, s, pt: (layer, pt[b * n_pages + s * group + g], 0, 0))

    per_b = lambda r, c: pl.BlockSpec((None, r, c), lambda b, s, pt: (b, 0, 0))
    kern = functools.partial(_fox_sample_kernel, group=group, page=page, t_new=t_new)
    return pl.pallas_call(
        kern,
        out_shape=jax.ShapeDtypeStruct((bd, t_new, ATTN_WIDTH), BF16),
        grid_spec=pltpu.PrefetchScalarGridSpec(
            num_scalar_prefetch=1,
            grid=(bd, steps),
            in_specs=[per_b(t_new, ATTN_WIDTH), per_b(rows, 1),
                      pl.BlockSpec((None, N_HEADS, group * page), lambda b, s, pt: (b, 0, s)),
                      per_b(t_new, ATTN_WIDTH), per_b(t_new, ATTN_WIDTH), per_b(rows, t_new)]
                     + [page_spec(g) for g in range(group)] * 2,
            out_specs=per_b(t_new, ATTN_WIDTH),
            scratch_shapes=[pltpu.VMEM((rows, ATTN_WIDTH), BF16), pltpu.VMEM((rows, 1), F32),
                            pltpu.VMEM((rows, 1), F32), pltpu.VMEM((rows, ATTN_WIDTH), F32)],
        ),
        compiler_params=_params(2),
        name="fox_sample",
    )(page_table.reshape(-1), q, cq_rows, ck_rows, k_new, v_new, bias_new,
      *([cache_kt] * group), *([cache_vt] * group))


def _top2(logits):
    lane = lax.broadcasted_iota(jnp.int32, logits.shape, 1)
    lane_f = lane.astype(F32)
    neg = jnp.float32(-jnp.inf)
    x = jnp.where(lane < N_EXPERTS, logits, neg)
    v1 = jnp.max(x, axis=1, keepdims=True)
    i1 = jnp.min(jnp.where(x == v1, lane_f, float(LANES)), axis=1, keepdims=True)
    x2 = jnp.where(lane_f == i1, neg, x)
    v2 = jnp.max(x2, axis=1, keepdims=True)
    i2 = jnp.min(jnp.where(x2 == v2, lane_f, float(LANES)), axis=1, keepdims=True)
    e2 = jnp.exp(v2 - v1)
    g1 = 1.0 / (1.0 + e2)
    g2 = e2 / (1.0 + e2)
    return i1.astype(jnp.int32), i2.astype(jnp.int32), g1, g2


def _out_proj_kernel(x_ref, ya_ref, yp_ref, yc_ref, wa_ref, wp_ref, wc_ref, *rest, route):
    acc = jnp.dot(ya_ref[...], wa_ref[...], preferred_element_type=F32)
    acc = acc + jnp.dot(yp_ref[...], wp_ref[...], preferred_element_type=F32)
    acc = acc + jnp.dot(yc_ref[...], wc_ref[...], preferred_element_type=F32)
    xn = x_ref[...] + acc
    if not route:
        (o_ref,) = rest
        o_ref[...] = xn
        return
    g_ref, wr_ref, o_ref, eid_ref, gate_ref = rest
    o_ref[...] = xn
    h = _rms_norm(xn, g_ref[...])
    h_hi = h.astype(BF16)
    h_lo = (h - h_hi.astype(F32)).astype(BF16)
    w_hi, w_lo = wr_ref[0], wr_ref[1]
    logits = (jnp.dot(h_hi, w_hi, preferred_element_type=F32) + jnp.dot(h_hi, w_lo, preferred_element_type=F32)
              + jnp.dot(h_lo, w_hi, preferred_element_type=F32))
    i1, i2, g1, g2 = _top2(logits)
    lane = lax.broadcasted_iota(jnp.int32, logits.shape, 1)
    eid_ref[...] = jnp.where(lane == 0, i1, i2)[:, :2]
    gate_ref[...] = jnp.where(lane == 0, g1, g2)[:, :2]


def _out_proj(x, ya, yp, yc, wa, wp, wc, tm, router=None):
    n = x.shape[0]
    row = lambda c: pl.BlockSpec((tm, c), lambda i: (i, 0))
    full = lambda a: pl.BlockSpec(a.shape, lambda i: (0,) * a.ndim)
    ins = [x, ya, yp, yc, wa, wp, wc]
    in_specs = [row(D_MODEL), row(ATTN_WIDTH), row(POOL_WIDTH), row(CONV_WIDTH), full(wa), full(wp), full(wc)]
    outs = [jax.ShapeDtypeStruct((n, D_MODEL), F32)]
    out_specs = [row(D_MODEL)]
    if router is not None:
        ins += list(router)
        in_specs += [full(a) for a in router]
        outs += [jax.ShapeDtypeStruct((n, 2), jnp.int32), jax.ShapeDtypeStruct((n, 2), F32)]
        out_specs += [row(2), row(2)]
    return pl.pallas_call(
        functools.partial(_out_proj_kernel, route=router is not None),
        out_shape=outs,
        grid=(n // tm,),
        in_specs=in_specs,
        out_specs=out_specs,
        compiler_params=_params(1),
        name="out_proj",
    )(*ins)


FFN_CHUNKS = 2


def _swiglu(x, g, wg_ref, wu_ref, wd_ref):
    h = _rms_norm(x, g).astype(BF16)
    fc = wg_ref.shape[-1] // FFN_CHUNKS
    acc = None
    for c in range(FFN_CHUNKS):
        sl = slice(c * fc, (c + 1) * fc)
        gate = jnp.dot(h, wg_ref[:, sl], preferred_element_type=F32)
        up = jnp.dot(h, wu_ref[:, sl], preferred_element_type=F32)
        hid = (gate * jax.nn.sigmoid(gate) * up).astype(BF16)
        part = jnp.dot(hid, wd_ref[sl, :], preferred_element_type=F32)
        acc = part if acc is None else acc + part
    return acc


def _ffn_dense_kernel(x_ref, g_ref, wg_ref, wu_ref, wd_ref, *rest, final_norm):
    o_ref = rest[-1]
    x = x_ref[...]
    y = x + _swiglu(x, g_ref[...], wg_ref, wu_ref, wd_ref)
    if final_norm:
        y = _rms_norm(y, rest[0][...])
    o_ref[...] = y


def _ffn_dense(x, g, wg, wu, wd, tm, final_g=None):
    n = x.shape[0]
    row = pl.BlockSpec((tm, D_MODEL), lambda i: (i, 0))
    resident = lambda a: pl.BlockSpec(a.shape, lambda i: (0,) * a.ndim, pipeline_mode=pl.Buffered(1))
    vec = pl.BlockSpec((1, D_MODEL), lambda i: (0, 0))
    ins = [x, g, wg, wu, wd]
    in_specs = [row, vec, resident(wg), resident(wu), resident(wd)]
    if final_g is not None:
        ins.append(final_g)
        in_specs.append(vec)
    return pl.pallas_call(
        functools.partial(_ffn_dense_kernel, final_norm=final_g is not None),
        out_shape=jax.ShapeDtypeStruct((n, D_MODEL), F32),
        grid=(n // tm,),
        in_specs=in_specs,
        out_specs=row,
        compiler_params=_params(1),
        name="ffn_dense",
    )(*ins)


def _ffn_experts_kernel(src_ref, te_ref, nv_ref, x_hbm, g_ref, wg_ref, wu_ref, wd_ref, o_ref, xbuf, sems, *, tm):
    i = pl.program_id(0)
    n_valid = nv_ref[0]

    def row_copy(tile, r, src_row):
        slot = tile % 2
        return pltpu.make_async_copy(x_hbm.at[pl.ds(src_row, 1), :], xbuf.at[slot, pl.ds(r, 1), :], sems.at[slot])

    def gather(tile):
        for r in range(tm):
            row_copy(tile, r, src_ref[tile * tm + r]).start()

    @pl.when(jnp.logical_and(i == 0, n_valid > 0))
    def _():
        gather(0)

    @pl.when(i + 1 < n_valid)
    def _():
        gather(i + 1)

    @pl.when(i < n_valid)
    def _():
        def wait(r, _):
            row_copy(i, r, 0).wait()
            return 0
        lax.fori_loop(0, tm, wait, 0, unroll=8)
        o_ref[...] = _swiglu(xbuf[i % 2], g_ref[...], wg_ref, wu_ref, wd_ref)

    @pl.when(i >= n_valid)
    def _():
        o_ref[...] = jnp.zeros(o_ref.shape, F32)


def _ffn_experts(x_all, src, tile_expert, n_valid, g, wg, wu, wd, tm):
    n_slots = src.shape[0]
    d_ff = wg.shape[-1]
    row = pl.BlockSpec((tm, D_MODEL), lambda i, src, te, nv: (i, 0))
    w_in = pl.BlockSpec((None, D_MODEL, d_ff), lambda i, src, te, nv: (te[i], 0, 0), pipeline_mode=pl.Buffered(1))
    w_out = pl.BlockSpec((None, d_ff, D_MODEL), lambda i, src, te, nv: (te[i], 0, 0), pipeline_mode=pl.Buffered(1))
    vec = pl.BlockSpec((1, D_MODEL), lambda i, src, te, nv: (0, 0))
    return pl.pallas_call(
        functools.partial(_ffn_experts_kernel, tm=tm),
        out_shape=jax.ShapeDtypeStruct((n_slots, D_MODEL), F32),
        grid_spec=pltpu.PrefetchScalarGridSpec(
            num_scalar_prefetch=3,
            grid=(n_slots // tm,),
            in_specs=[pl.BlockSpec(memory_space=pl.ANY), vec, w_in, w_in, w_out],
            out_specs=row,
            scratch_shapes=[pltpu.VMEM((2, tm, D_MODEL), F32), pltpu.SemaphoreType.DMA((2,))],
        ),
        compiler_params=_params(1),
        name="ffn_experts",
    )(src, tile_expert, n_valid, x_all, g, wg, wu, wd)


def _combine_kernel(pos_ref, x_ref, gate_ref, ys_ref, *rest, tm, final_norm):
    o_ref, buf, sems = rest[-3:]
    i = pl.program_id(0)

    def row_copy(tile, r, k, src_row):
        slot = tile % 2
        return pltpu.make_async_copy(ys_ref.at[pl.ds(src_row, 1), :], buf.at[slot, k, pl.ds(r, 1), :],
                                     sems.at[slot])

    def gather(tile):
        for r in range(tm):
            for k in range(2):
                row_copy(tile, r, k, pos_ref[2 * tile * tm + 2 * r + k]).start()

    @pl.when(i == 0)
    def _():
        gather(0)

    @pl.when(i + 1 < pl.num_programs(0))
    def _():
        gather(i + 1)

    def wait(r, _):
        for k in range(2):
            row_copy(i, r, k, 0).wait()
        return 0

    lax.fori_loop(0, tm, wait, 0, unroll=4)
    gate = gate_ref[...]
    slot = i % 2
    moe = gate[:, 0:1] * buf[slot, 0] + gate[:, 1:2] * buf[slot, 1]
    y = x_ref[...] + moe
    if final_norm:
        y = _rms_norm(y, rest[0][...])
    o_ref[...] = y


def _combine(x, gates, pos, ys, tm, final_g=None):
    n = x.shape[0]
    row = lambda c: pl.BlockSpec((tm, c), lambda i, pos: (i, 0))
    ins = [x, gates, ys]
    in_specs = [row(D_MODEL), row(2), pl.BlockSpec(memory_space=pl.ANY)]
    if final_g is not None:
        ins.append(final_g)
        in_specs.append(pl.BlockSpec((1, D_MODEL), lambda i, pos: (0, 0)))
    return pl.pallas_call(
        functools.partial(_combine_kernel, tm=tm, final_norm=final_g is not None),
        out_shape=jax.ShapeDtypeStruct((n, D_MODEL), F32),
        grid_spec=pltpu.PrefetchScalarGridSpec(
            num_scalar_prefetch=1,
            grid=(n // tm,),
            in_specs=in_specs,
            out_specs=row(D_MODEL),
            scratch_shapes=[pltpu.VMEM((2, 2, tm, D_MODEL), F32), pltpu.SemaphoreType.DMA((2,))],
        ),
        compiler_params=_params(1),
        name="moe_combine",
    )(pos, *ins)


def _route(eid, tm):
    n_pairs = eid.shape[0] * 2
    flat = eid.reshape(-1)
    onehot = (flat[:, None] == jnp.arange(N_EXPERTS, dtype=jnp.int32)[None, :]).astype(jnp.int32)
    csum = jnp.cumsum(onehot, axis=0)
    rank = jnp.sum((csum - onehot) * onehot, axis=1)
    counts = csum[-1]
    padded = ((counts + tm - 1) // tm) * tm
    ends = jnp.cumsum(padded)
    starts = ends - padded
    pos = (starts[flat] + rank).astype(jnp.int32)
    n_tiles = -(-n_pairs // tm) + N_EXPERTS
    tile_start = jnp.arange(n_tiles, dtype=jnp.int32) * tm
    tile_expert = jnp.minimum(jnp.sum(tile_start[:, None] >= ends[None, :], axis=1), N_EXPERTS - 1).astype(jnp.int32)
    n_valid = (ends[-1] // tm).astype(jnp.int32).reshape(1)
    src = jnp.zeros((n_tiles * tm,), jnp.int32).at[pos].set(jnp.arange(n_pairs, dtype=jnp.int32) // 2)
    return pos, src, tile_expert, n_valid


def _block_diag(pool_w):
    g, c, _ = pool_w.shape
    out = jnp.zeros((g * c, g * c), pool_w.dtype)
    for i in range(g):
        out = out.at[i * c:(i + 1) * c, i * c:(i + 1) * c].set(pool_w[i])
    return out


def _split_w_in(w_in):
    a_w = ATTN_WIDTH
    q, k, v = w_in[:, :a_w], w_in[:, a_w:2 * a_w], w_in[:, 2 * a_w:3 * a_w]
    f = w_in[:, 3 * a_w:3 * a_w + N_HEADS]
    rest = w_in[:, 3 * a_w + N_HEADS:]
    w_rows = jnp.concatenate([q, v, rest], axis=1).astype(BF16)
    pad = jnp.zeros((D_MODEL, 16 - N_HEADS), w_in.dtype)
    w_t = jnp.concatenate([k, v, f, pad], axis=1).T.astype(BF16)
    return w_rows, w_t


def kernel(x_prompt, x_sample, cache_k, cache_v, cache_logf, state_pool, state_conv, page_table, norm_mix, w_in, b_forget, pool_w, pool_scale, conv_w, conv_b, conv_ln_g, conv_ln_b, w_out, norm_ffn, ffn_gate, ffn_up, ffn_down, moe_router, moe_gate, moe_up, moe_down, norm_final):
    batch, seq, _ = x_prompt.shape
    bd, t_new, _ = x_sample.shape
    depth = w_in.shape[0]
    n_p, n_s = batch * seq, bd * t_new
    n_pages, page = page_table.shape[1], cache_k.shape[2]
    past = n_pages * page
    tm_p, tm_s = 512, n_s
    tq, tk = 512, 512
    ts = 512
    page_group = 16
    tm_e = 512
    assert tq % tk == 0 and seq % tq == 0 and seq % tm_p == 0 and n_pages % page_group == 0

    xp = x_prompt.reshape(n_p, D_MODEL)
    xs = x_sample.reshape(n_s, D_MODEL)
    row2 = lambda v: v.reshape(1, -1).astype(F32)
    n_pool = cache_k.shape[1]
    cache_kt = cache_k.transpose(0, 1, 3, 4, 2).reshape(depth, n_pool, ATTN_WIDTH, page)
    cache_vt = cache_v.transpose(0, 1, 3, 4, 2).reshape(depth, n_pool, ATTN_WIDTH, page)
    cache_lft = cache_logf.transpose(0, 1, 3, 2)
    new_p = [[] for _ in range(5)]
    new_s = [[] for _ in range(5)]
    for l in range(depth):
        last = l == depth - 1
        g_mix = row2(norm_mix[l])
        w_rows, w_t = _split_w_in(w_in[l])
        bf_col = b_forget[l].astype(F32).reshape(N_HEADS, 1)
        pool_bd = _block_diag(pool_w[l]).astype(BF16)
        ps, cw, cb = row2(pool_scale[l]), conv_w[l].astype(F32), row2(conv_b[l])
        lg, lb = row2(conv_ln_g[l]), row2(conv_ln_b[l])
        wo = w_out[l].astype(BF16)
        wa, wp, wc = wo[:ATTN_WIDTH], wo[ATTN_WIDTH:ATTN_WIDTH + POOL_WIDTH], wo[ATTN_WIDTH + POOL_WIDTH:]

        stack_here = last and l > 0
        q, vb, u, a, kt, vt, ktb, lft, kmax, y_pool, y_conv = _in_proj(
            xp, g_mix, w_rows, w_t, bf_col, batch, seq, tm_p, mix_weights=(pool_bd, ps, cw, cb, lg, lb),
            earlier=tuple(new_p[:3]) if stack_here else None)
        if stack_here:
            state_p = (kt, vt, lft)
            lft = lft[l]
        c_row = _cumsum_last(lft.reshape(batch * N_HEADS, seq)).reshape(batch, N_HEADS, seq)
        k_norm_max = jnp.max(kmax[:, :, 0].reshape(batch, seq // tm_p, N_HEADS), axis=1)
        y_att = _fox_prompt(q, ktb, vb, c_row, k_norm_max, batch, seq, tq, tk)
        new_p[0].append(kt)
        new_p[1].append(vt)
        new_p[2].append(lft)
        new_p[3].append(u.reshape(batch, seq, POOL_WIDTH)[:, seq - POOL_HIST:])
        new_p[4].append(a.reshape(batch, seq, CONV_WIDTH)[:, seq - CONV_HIST:])

        qs, _, us, a_s, kts, vts, _, lfts, _ = _in_proj(xs, g_mix, w_rows, w_t, bf_col, 1, n_s, tm_s)
        kfs, vfs, lfs = kts[0].T, vts[0].T, lfts[0].T
        lf_group = 2 * page_group if n_pages % (2 * page_group) == 0 else page_group
        lf_past = _gather_logf_pages(cache_lft, l, page_table, lf_group)
        c_past = _cumsum_last(lf_past.reshape(bd * N_HEADS, past), neg_suffix=True).reshape(bd, N_HEADS, past)
        lf_new = lfs.reshape(bd, t_new, N_HEADS)
        c_new = jnp.cumsum(lf_new, axis=1)
        cq_rows = c_new.reshape(bd, t_new * N_HEADS, 1)
        tri = jnp.arange(t_new)[None, :] <= jnp.arange(t_new)[:, None]
        bias_new = c_new[:, :, None, :] - c_new[:, None, :, :]
        bias_new = jnp.where(tri[None, :, :, None], bias_new, -jnp.inf)
        bias_new = bias_new.transpose(0, 1, 3, 2).reshape(bd, t_new * N_HEADS, t_new)
        y_att_s = _fox_sample(qs.reshape(bd, t_new, ATTN_WIDTH), kfs.reshape(bd, t_new, ATTN_WIDTH),
                              vfs.reshape(bd, t_new, ATTN_WIDTH), cq_rows, c_past, bias_new,
                              cache_kt, cache_vt, l, page_table, page_group).reshape(n_s, ATTN_WIDTH)
        ext_pool = jnp.concatenate([state_pool[l].astype(F32), us.reshape(bd, t_new, POOL_WIDTH)], axis=1)
        ext_conv = jnp.concatenate([state_conv[l].astype(F32), a_s.reshape(bd, t_new, CONV_WIDTH)], axis=1)
        y_pool_s, y_conv_s = _sample_mix(ext_pool, ext_conv, pool_bd, ps, cw, cb, lg, lb, t_new)
        new_s[0].append(kfs.reshape(bd, t_new, N_HEADS, HEAD_DIM))
        new_s[1].append(vfs.reshape(bd, t_new, N_HEADS, HEAD_DIM))
        new_s[2].append(lf_new)
        new_s[3].append(ext_pool[:, t_new:])
        new_s[4].append(ext_conv[:, t_new:])

        g_ffn = row2(norm_ffn[l])
        final_g = row2(norm_final) if last else None
        i = l // 2
        if l % 2 == 0:
            (xp,) = _out_proj(xp, y_att, y_pool, y_conv, wa, wp, wc, tm_p)
            (xs,) = _out_proj(xs, y_att_s, y_pool_s, y_conv_s, wa, wp, wc, tm_s)
            wg, wu, wd = (w[i].astype(BF16) for w in (ffn_gate, ffn_up, ffn_down))
            xp = _ffn_dense(xp, g_ffn, wg, wu, wd, tm_p, final_g)
            xs = _ffn_dense(xs, g_ffn, wg, wu, wd, tm_s, final_g)
        else:
            wr = jnp.concatenate([moe_router[i].astype(F32), jnp.zeros((D_MODEL, LANES - N_EXPERTS), F32)], axis=1)
            wr_hi = wr.astype(BF16)
            wr = jnp.stack([wr_hi, (wr - wr_hi.astype(F32)).astype(BF16)])
            xp, eid_p, gate_p = _out_proj(xp, y_att, y_pool, y_conv, wa, wp, wc, tm_p, router=(g_ffn, wr))
            xs, eid_s, gate_s = _out_proj(xs, y_att_s, y_pool_s, y_conv_s, wa, wp, wc, tm_s, router=(g_ffn, wr))
            pos, src, tile_expert, n_valid = _route(jnp.concatenate([eid_p, eid_s], axis=0), tm_e)
            wg, wu, wd = (w[i].astype(BF16) for w in (moe_gate, moe_up, moe_down))
            y_sorted = _ffn_experts(jnp.concatenate([xp, xs], axis=0), src, tile_expert, n_valid,
                                    g_ffn, wg, wu, wd, tm_e)
            xp = _combine(xp, gate_p, pos[:2 * n_p], y_sorted, 256, final_g)
            xs = _combine(xs, gate_s, pos[2 * n_p:], y_sorted, n_s, final_g)
    y_prompt = xp.reshape(batch, seq, D_MODEL)
    y_sample = xs.reshape(bd, t_new, D_MODEL)
    pool_p, conv_p = jnp.stack(new_p[3], axis=0), jnp.stack(new_p[4], axis=0)
    kt_p, vt_p, lft_p = state_p if depth > 1 else [jnp.stack(a, axis=0) for a in new_p[:3]]
    heads_last = lambda t: t.reshape(depth, batch, N_HEADS, HEAD_DIM, seq).transpose(0, 1, 4, 2, 3)
    k_p, v_p, lf_p = heads_last(kt_p), heads_last(vt_p), lft_p.transpose(0, 1, 3, 2)
    k_s, v_s, lf_s, pool_s, conv_s = [jnp.stack(a, axis=0) for a in new_s]
    return (y_prompt, y_sample, k_p, v_p, lf_p, pool_p, conv_p, k_s, v_s, lf_s, pool_s, conv_s)
```
